```python
import math
import jax, jax.numpy as jnp
from jax import lax
import numpy as np

D_MODEL = 2048
BATCH = 2
SEQ = 8192
DEPTH = 4

N_META = 16
BLOCK = 128
META_PAD = BLOCK - N_META
EPS = 1e-6

HEAD_DIM = 128
MIX_WIDTH = D_MODEL
POOL_WIDTH = MIX_WIDTH // 4
ML_WIDTH = (MIX_WIDTH - POOL_WIDTH) // 2
SB_WIDTH = MIX_WIDTH - POOL_WIDTH - ML_WIDTH

ML_HEADS = ML_WIDTH // HEAD_DIM
ML_DV = HEAD_DIM
ML_DQK = HEAD_DIM // 2
ML_QK_WIDTH = ML_HEADS * ML_DQK
ML_CONV = 4

SB_HEADS = SB_WIDTH // HEAD_DIM
SB_DH = HEAD_DIM

POOL_WINDOWS = (2, 4, 8, 16)
POOL_GROUPS = len(POOL_WINDOWS)
POOL_CH = POOL_WIDTH // POOL_GROUPS

FFN_HIDDEN = -(-(8 * D_MODEL) // (3 * 256)) * 256

IN_SIZES = (ML_QK_WIDTH, ML_QK_WIDTH, ML_WIDTH, ML_WIDTH, ML_HEADS, ML_HEADS,
            SB_WIDTH, SB_WIDTH, SB_WIDTH, POOL_WIDTH)
IN_WIDTH = sum(IN_SIZES)

kernel_name = "hymba_mlstm_stickbreak_pool_hybrid"


def rmsnorm(x, g):
    xf = x.astype(jnp.float32)
    y = xf * lax.rsqrt(jnp.mean(xf * xf, axis=-1, keepdims=True) + EPS)
    return (y * g.astype(jnp.float32)).astype(x.dtype)


def split_cols(u, sizes):
    out, off = [], 0
    for s in sizes:
        out.append(u[..., off:off + s])
        off += s
    return out


def to_heads(x, n_heads):
    B, L, _ = x.shape
    return x.reshape(B, L, n_heads, -1).transpose(0, 2, 1, 3)


def from_heads(x):
    B, H, L, d = x.shape
    return x.transpose(0, 2, 1, 3).reshape(B, L, H * d)


def causal_dwconv(x, w):
    K, C = w.shape
    return lax.conv_general_dilated(
        x, w[:, None, :].astype(x.dtype), window_strides=(1,), padding=[(K - 1, 0)],
        dimension_numbers=('NWC', 'WIO', 'NWC'), feature_group_count=C)


def mlstm_chunkwise(q, k, v, log_i, log_f):
    f32 = jnp.float32
    B, H, L, dqk = q.shape
    dv = v.shape[-1]
    nc = L // BLOCK
    qc = q.astype(f32).reshape(B, H, nc, BLOCK, dqk)
    kc = (k.astype(f32) / math.sqrt(dqk)).reshape(B, H, nc, BLOCK, dqk)
    vc = v.astype(f32).reshape(B, H, nc, BLOCK, dv)
    ig = log_i.reshape(B, H, nc, BLOCK)
    b = jnp.cumsum(log_f.reshape(B, H, nc, BLOCK), axis=-1)
    b_last = b[..., -1]

    g = b_last[..., None] - b + ig
    m_loc = jnp.max(g, axis=-1)
    w_loc = jnp.exp(g - m_loc[..., None])
    C_loc = jnp.einsum('bhcsv,bhcsk->bhcvk', w_loc[..., None] * vc, kc)
    n_loc = jnp.einsum('bhcs,bhcsk->bhck', w_loc, kc)

    def step(carry, xs):
        C, n, m = carry
        Cl, nl, ml, bl = xs
        m_new = jnp.maximum(bl + m, ml)
        a = jnp.exp(bl + m - m_new)
        c = jnp.exp(ml - m_new)
        C_new = a[..., None, None] * C + c[..., None, None] * Cl
        n_new = a[..., None] * n + c[..., None] * nl
        return (C_new, n_new, m_new), (C, n, m)

    init = (jnp.zeros((B, H, dv, dqk), f32), jnp.zeros((B, H, dqk), f32), jnp.zeros((B, H), f32))
    xs = (jnp.moveaxis(C_loc, 2, 0), jnp.moveaxis(n_loc, 2, 0),
          jnp.moveaxis(m_loc, 2, 0), jnp.moveaxis(b_last, 2, 0))
    _, (C_prev, n_prev, m_prev) = lax.scan(step, init, xs)
    C_prev = jnp.moveaxis(C_prev, 0, 2)
    n_prev = jnp.moveaxis(n_prev, 0, 2)
    m_prev = jnp.moveaxis(m_prev, 0, 2)

    causal = jnp.tril(jnp.ones((BLOCK, BLOCK), bool))
    Dm = jnp.where(causal, b[..., :, None] - b[..., None, :] + ig[..., None, :], -jnp.inf)
    inter_log = b + m_prev[..., None]
    m_t = jnp.maximum(inter_log, jnp.max(Dm, axis=-1))
    S = jnp.einsum('bhctk,bhcsk->bhcts', qc, kc) * jnp.exp(Dm - m_t[..., None])
    a_inter = jnp.exp(inter_log - m_t)
    num = (jnp.einsum('bhcts,bhcsv->bhctv', S, vc)
           + a_inter[..., None] * jnp.einsum('bhcvk,bhctk->bhctv', C_prev, qc))
    den = jnp.sum(S, axis=-1) + a_inter * jnp.einsum('bhck,bhctk->bhct', n_prev, qc)
    hm = num / jnp.maximum(jnp.abs(den), jnp.exp(-m_t))[..., None]
    return hm.reshape(B, H, L, dv)


def stick_breaking(q, k, v, key_valid):
    f32 = jnp.float32
    B, H, L, d = q.shape
    nb = L // BLOCK
    scale = 1.0 / math.sqrt(d)
    kf = k.astype(f32)
    vf = v.astype(f32)
    kpos = jnp.arange(L)
    qb = jnp.moveaxis(q.astype(f32).reshape(B, H, nb, BLOCK, d), 2, 0)

    def block(args):
        q_blk, blk = args
        qpos = blk * BLOCK + jnp.arange(BLOCK)
        z = jnp.einsum('bhtd,bhsd->bhts', q_blk, kf) * scale
        live = (kpos[None, :] < qpos[:, None]) & key_valid[None, :]
        log_beta = jnp.where(live, jax.nn.log_sigmoid(z), -jnp.inf)
        log_1mb = jnp.where(live, jax.nn.log_sigmoid(-z), 0.0)
        later = lax.cumsum(log_1mb, axis=3, reverse=True) - log_1mb
        A = jnp.exp(log_beta + later)
        return jnp.einsum('bhts,bhsd->bhtd', A, vf)

    out = lax.map(block, (qb, jnp.arange(nb)))
    return jnp.moveaxis(out, 0, 2).reshape(B, H, L, d)


def multiscale_pool(u, valid, w_pool, pool_scale):
    f32 = jnp.float32
    B, L, _ = u.shape
    vf = valid.astype(f32)
    uf = u.astype(f32) * vf[None, :, None]
    maxw = max(POOL_WINDOWS)
    cs = jnp.pad(jnp.cumsum(uf, axis=1), ((0, 0), (maxw, 0), (0, 0)))
    cnt = jnp.pad(jnp.cumsum(vf), (maxw, 0))
    outs = []
    for g, w in enumerate(POOL_WINDOWS):
        lo, hi = g * POOL_CH, (g + 1) * POOL_CH
        s = cs[:, maxw:, lo:hi] - cs[:, maxw - w:maxw - w + L, lo:hi]
        c = cnt[maxw:] - cnt[maxw - w:maxw - w + L]
        outs.append(s / jnp.maximum(c, 1.0)[None, :, None] - uf[..., lo:hi])
    y = jnp.stack(outs, axis=2)
    y = jnp.einsum('blgc,gcd->blgd', y, w_pool.astype(f32)) * pool_scale.astype(f32)
    return y.reshape(B, L, POOL_WIDTH)


def token_mixing(h, valid, w_in, conv_w, ig_b, fg_b, pool_w, pool_scale, w_out, g_pre, g_post):
    f32 = jnp.float32
    xn = rmsnorm(h, g_pre)
    vmask = valid[None, :, None]
    u = jnp.einsum('bld,de->ble', xn, w_in) * vmask.astype(h.dtype)
    ml_q, ml_k, ml_v, ml_o, ml_i, ml_f, sb_q, sb_k, sb_v, pool_u = split_cols(u, IN_SIZES)

    qk = jax.nn.silu(causal_dwconv(jnp.concatenate([ml_q, ml_k], axis=-1), conv_w))
    ml_q, ml_k = qk[..., :ML_QK_WIDTH], qk[..., ML_QK_WIDTH:]
    log_i = jnp.where(vmask, ml_i.astype(f32) + ig_b.astype(f32), -jnp.inf)
    log_f = jnp.where(vmask, jax.nn.log_sigmoid(ml_f.astype(f32) + fg_b.astype(f32)), 0.0)
    hm = mlstm_chunkwise(to_heads(ml_q, ML_HEADS), to_heads(ml_k, ML_HEADS), to_heads(ml_v, ML_HEADS),
                         log_i.transpose(0, 2, 1), log_f.transpose(0, 2, 1))
    ml_out = from_heads(hm).astype(h.dtype) * jax.nn.sigmoid(ml_o)

    sb = stick_breaking(to_heads(sb_q, SB_HEADS), to_heads(sb_k, SB_HEADS), to_heads(sb_v, SB_HEADS), valid)
    sb_out = from_heads(sb).astype(h.dtype)

    pool_out = multiscale_pool(pool_u, valid, pool_w, pool_scale).astype(h.dtype)

    mix = jnp.concatenate([ml_out, sb_out, pool_out], axis=-1)
    y = jnp.einsum('ble,ed->bld', mix, w_out)
    return rmsnorm(y, g_post)


def channel_mixing(h, w_gate_up, w_down, g_pre, g_post):
    xn = rmsnorm(h, g_pre)
    gu = jnp.einsum('bld,df->blf', xn, w_gate_up)
    gate, up = gu[..., :FFN_HIDDEN], gu[..., FFN_HIDDEN:]
    y = jnp.einsum('blf,fd->bld', jax.nn.silu(gate) * up, w_down)
    return rmsnorm(y, g_post)


def setup_inputs(seed: int = 0) -> dict:
    key = jax.random.key(seed)
    ks = jax.random.split(key, 16)
    f32 = jnp.float32
    nrm = lambda k, shape, s: jax.random.normal(k, shape, f32) * s
    gain = lambda k: 1.0 + nrm(k, (DEPTH, D_MODEL), 0.02)
    fg_b = jnp.tile(jnp.linspace(3.0, 6.0, ML_HEADS, dtype=f32)[None, :], (DEPTH, 1)) + nrm(ks[5], (DEPTH, ML_HEADS), 0.1)
    return {
        'x': nrm(ks[0], (BATCH, SEQ, D_MODEL), 1.0),
        'meta_tokens': nrm(ks[1], (N_META, D_MODEL), 1.0),
        'w_in': nrm(ks[2], (DEPTH, D_MODEL, IN_WIDTH), D_MODEL ** -0.5),
        'ml_conv_w': nrm(ks[3], (DEPTH, ML_CONV, 2 * ML_QK_WIDTH), ML_CONV ** -0.5),
        'ml_igate_b': nrm(ks[4], (DEPTH, ML_HEADS), 0.1),
        'ml_fgate_b': fg_b,
        'pool_w': nrm(ks[6], (DEPTH, POOL_GROUPS, POOL_CH, POOL_CH), POOL_CH ** -0.5),
        'pool_scale': 1.0 + nrm(ks[7], (DEPTH, POOL_GROUPS, POOL_CH), 0.02),
        'w_out': nrm(ks[8], (DEPTH, MIX_WIDTH, D_MODEL), MIX_WIDTH ** -0.5),
        'g_mix_pre': gain(ks[9]),
        'g_mix_post': gain(ks[10]),
        'g_ffn_pre': gain(ks[11]),
        'g_ffn_post': gain(ks[12]),
        'w_gate_up': nrm(ks[13], (DEPTH, D_MODEL, 2 * FFN_HIDDEN), D_MODEL ** -0.5),
        'w_down': nrm(ks[14], (DEPTH, FFN_HIDDEN, D_MODEL), FFN_HIDDEN ** -0.5),
    }


def reference(x, meta_tokens, w_in, ml_conv_w, ml_igate_b, ml_fgate_b, pool_w, pool_scale, w_out,
              g_mix_pre, g_mix_post, g_ffn_pre, g_ffn_post, w_gate_up, w_down):
    B = x.shape[0]
    meta = jnp.broadcast_to(meta_tokens[None].astype(x.dtype), (B, N_META, D_MODEL))
    h = jnp.concatenate([jnp.zeros((B, META_PAD, D_MODEL), x.dtype), meta, x], axis=1)
    L = h.shape[1]
    valid = jnp.arange(L) >= META_PAD
    for l in range(DEPTH):
        h = h + token_mixing(h, valid, w_in[l], ml_conv_w[l], ml_igate_b[l], ml_fgate_b[l],
                             pool_w[l], pool_scale[l], w_out[l], g_mix_pre[l], g_mix_post[l])
        h = h + channel_mixing(h, w_gate_up[l], w_down[l], g_ffn_pre[l], g_ffn_post[l])
    return h[:, BLOCK:, :]
```

```python
import functools
import math

import jax
import jax.numpy as jnp
from jax import lax
from jax.experimental import pallas as pl
from jax.experimental.pallas import tpu as pltpu

F32 = jnp.float32
BF16 = jnp.bfloat16

D_MODEL = 2048
N_META = 16
BLOCK = 128
META_PAD = BLOCK - N_META
EPS = 1e-6

HEAD_DIM = 128
ML_HEADS = 6
ML_DQK = 64
ML_QK_WIDTH = ML_HEADS * ML_DQK
ML_WIDTH = ML_HEADS * HEAD_DIM
ML_CONV = 4
SB_HEADS = 6
SB_WIDTH = SB_HEADS * HEAD_DIM
POOL_WINDOWS = (2, 4, 8, 16)
POOL_CH = 128
POOL_WIDTH = POOL_CH * len(POOL_WINDOWS)
POOL_HALO = 16
FFN_HIDDEN = 5632
IN_SIZES = (ML_QK_WIDTH, ML_QK_WIDTH, ML_WIDTH, ML_WIDTH, ML_HEADS, ML_HEADS,
            SB_WIDTH, SB_WIDTH, SB_WIDTH, POOL_WIDTH)

SEG = 768
COL_QK, COL_MLV, COL_MLO, COL_SBQ, COL_SBK, COL_SBV = (i * SEG for i in range(6))
COL_POOL = 6 * SEG
COL_GATE = COL_POOL + POOL_WIDTH
GATE_W = 256
N_IN = COL_GATE + GATE_W

V7X_VMEM_LIMIT = 52 * 1024 * 1024

SB_DEAD_LOG = -104.0


def _rms_scale(x, g):
    ms = jnp.mean(x * x, axis=-1, keepdims=True)
    return x * lax.rsqrt(ms + EPS) * g


def _in_proj_kernel(h_ref, g_ref, w_ref, u_ref, xn_ref, *, tm, seq_len):
    i = pl.program_id(0)

    @pl.when(pl.program_id(1) == 0)
    def _():
        xn_ref[...] = _rms_scale(h_ref[...], g_ref[...]).astype(BF16)

    acc = jnp.dot(xn_ref[...], w_ref[...], preferred_element_type=F32)
    pos = lax.rem(i * tm, seq_len) + lax.broadcasted_iota(jnp.int32, (tm, 1), 0)
    u_ref[...] = acc * (pos >= META_PAD).astype(F32)


def _in_proj(h, g, w, *, seq_len, tm):
    m = h.shape[0]
    tn = SEG
    return pl.pallas_call(
        functools.partial(_in_proj_kernel, tm=tm, seq_len=seq_len),
        grid=(m // tm, N_IN // tn),
        in_specs=[
            pl.BlockSpec((tm, D_MODEL), lambda i, j: (i, 0)),
            pl.BlockSpec((1, D_MODEL), lambda i, j: (0, 0)),
            pl.BlockSpec((D_MODEL, tn), lambda i, j: (0, j)),
        ],
        out_specs=pl.BlockSpec((tm, tn), lambda i, j: (i, j)),
        out_shape=jax.ShapeDtypeStruct((m, N_IN), F32),
        scratch_shapes=[pltpu.VMEM((tm, D_MODEL), BF16)],
        compiler_params=pltpu.CompilerParams(
            dimension_semantics=("arbitrary", "arbitrary"),
            vmem_limit_bytes=V7X_VMEM_LIMIT),
        name="in_proj",
    )(h, g, w)


def _log_sigmoid(x):
    return jnp.minimum(x, 0.0) - jnp.log1p(jnp.exp(-jnp.abs(x)))


def _mlstm_kernel(qk_ref, v_ref, o_ref, gate_ref, convw_ref, gbias_ref, out_ref,
                  ext_ref, ct_ref, n_ref, m_ref):
    c = pl.program_id(1)
    T = BLOCK
    CONV_PAD = 8

    @pl.when(c == 0)
    def _():
        ext_ref[0:CONV_PAD, :] = jnp.zeros((CONV_PAD, 2 * ML_QK_WIDTH), F32)
        ct_ref[...] = jnp.zeros_like(ct_ref)
        n_ref[...] = jnp.zeros_like(n_ref)
        m_ref[...] = jnp.zeros_like(m_ref)

    ext_ref[CONV_PAD:CONV_PAD + T, :] = qk_ref[...]
    conv = jnp.zeros((T, 2 * ML_QK_WIDTH), F32)
    for j in range(ML_CONV):
        off = CONV_PAD - (ML_CONV - 1) + j
        conv = conv + convw_ref[j:j + 1, :] * ext_ref[off:off + T, :]
    ext_ref[0:CONV_PAD, :] = ext_ref[T:T + CONV_PAD, :]
    qk = conv * jax.nn.sigmoid(conv)
    q_all = qk[:, :ML_QK_WIDTH]
    k_all = qk[:, ML_QK_WIDTH:] * (1.0 / math.sqrt(ML_DQK))

    row = lax.broadcasted_iota(jnp.int32, (T, 1), 0)
    valid = (c * T + row) >= META_PAD
    gpre = gate_ref[:, 0:128] + gbias_ref[...]
    log_i = jnp.where(valid, gpre, -jnp.inf)
    log_f = jnp.where(valid, _log_sigmoid(gpre), 0.0)
    r_i = lax.broadcasted_iota(jnp.int32, (T, T), 0)
    c_i = lax.broadcasted_iota(jnp.int32, (T, T), 1)
    causal = c_i <= r_i
    tri = causal.astype(F32)
    bcum = jnp.dot(tri, log_f, preferred_element_type=F32, precision=lax.Precision.HIGHEST)
    bcum_t = bcum.T
    log_i_t = log_i.T

    lane = lax.broadcasted_iota(jnp.int32, (1, T), 1)
    outs = []
    for h in range(ML_HEADS):
        p, e = divmod(h, 2)
        half = (lane >= ML_DQK * e) & (lane < ML_DQK * (e + 1))
        q_pair = q_all[:, 128 * p:128 * (p + 1)]
        k_pair = k_all[:, 128 * p:128 * (p + 1)]
        qm = jnp.where(half, q_pair, 0.0).astype(BF16)
        v_h = v_ref[:, 128 * h:128 * (h + 1)]

        b_col = bcum[:, ML_HEADS + h:ML_HEADS + h + 1]
        ig_col = log_i[:, h:h + 1]
        a_row = log_i_t[h:h + 1, :] - bcum_t[ML_HEADS + h:ML_HEADS + h + 1, :]
        m_prev = m_ref[h:h + 1, 0:1]
        b_last = b_col[T - 1:T, :]

        dm = jnp.where(causal, b_col + a_row, -jnp.inf)
        inter = b_col + m_prev
        m_t = jnp.maximum(inter, jnp.max(dm, axis=1, keepdims=True))
        s_mat = lax.dot_general(qm, k_pair.astype(BF16), (((1,), (1,)), ((), ())),
                                preferred_element_type=F32) * jnp.exp(dm - m_t)
        a_inter = jnp.exp(inter - m_t)
        ct_pair = ct_ref[p]
        n_pair = n_ref[p:p + 1, :]
        num = (jnp.dot(s_mat.astype(BF16), v_h.astype(BF16), preferred_element_type=F32)
               + a_inter * jnp.dot(qm, ct_pair.astype(BF16), preferred_element_type=F32))
        qn = jnp.sum(qm.astype(F32) * n_pair, axis=1, keepdims=True)
        den = jnp.sum(s_mat, axis=1, keepdims=True) + a_inter * qn
        hm = num / jnp.maximum(jnp.abs(den), jnp.exp(-m_t))
        outs.append(hm * jax.nn.sigmoid(o_ref[:, 128 * h:128 * (h + 1)]))

        g_col = b_last + ig_col - b_col
        m_loc = jnp.max(g_col, axis=0, keepdims=True)
        w_loc = jnp.exp(g_col - m_loc)
        m_new = jnp.maximum(b_last + m_prev, m_loc)
        a_old = jnp.exp(b_last + m_prev - m_new)
        a_loc = jnp.exp(m_loc - m_new)
        ct_loc = jnp.dot(k_pair.T.astype(BF16), (w_loc * v_h).astype(BF16),
                         preferred_element_type=F32)
        n_loc = jnp.sum(w_loc * k_pair, axis=0, keepdims=True)
        lo, hi = ML_DQK * e, ML_DQK * (e + 1)
        ct_ref[p, lo:hi, :] = a_old * ct_pair[lo:hi, :] + a_loc * ct_loc[lo:hi, :]
        n_ref[p:p + 1, :] = jnp.where(half, a_old * n_pair + a_loc * n_loc, n_pair)
        m_ref[h:h + 1, :] = jnp.broadcast_to(m_new, (1, T))

    out_ref[...] = jnp.concatenate(outs, axis=1).astype(out_ref.dtype)


def _mlstm(u, conv_w, gate_bias, *, batch, seq_len):
    nc = seq_len // BLOCK
    row = lambda b, c: b * nc + c
    return pl.pallas_call(
        _mlstm_kernel,
        grid=(batch, nc),
        in_specs=[
            pl.BlockSpec((BLOCK, SEG), lambda b, c: (row(b, c), COL_QK // SEG)),
            pl.BlockSpec((BLOCK, SEG), lambda b, c: (row(b, c), COL_MLV // SEG)),
            pl.BlockSpec((BLOCK, SEG), lambda b, c: (row(b, c), COL_MLO // SEG)),
            pl.BlockSpec((BLOCK, GATE_W), lambda b, c: (row(b, c), COL_GATE // GATE_W)),
            pl.BlockSpec((ML_CONV, 2 * ML_QK_WIDTH), lambda b, c: (0, 0)),
            pl.BlockSpec((1, 128), lambda b, c: (0, 0)),
        ],
        out_specs=pl.BlockSpec((BLOCK, ML_WIDTH), lambda b, c: (row(b, c), 0)),
        out_shape=jax.ShapeDtypeStruct((batch * seq_len, ML_WIDTH), BF16),
        scratch_shapes=[
            pltpu.VMEM((8 + BLOCK, 2 * ML_QK_WIDTH), F32),
            pltpu.VMEM((ML_HEADS // 2, 128, 128), F32),
            pltpu.VMEM((8, 128), F32),
            pltpu.VMEM((8, 128), F32),
        ],
        compiler_params=pltpu.CompilerParams(
            dimension_semantics=("arbitrary", "arbitrary"),
            vmem_limit_bytes=V7X_VMEM_LIMIT),
        name="mlstm",
    )(u, u, u, u, conv_w, gate_bias)


def _sb_kernel(q_ref, k_ref, v_ref, out_ref):
    i = pl.program_id(2)
    T = BLOCK
    q = (q_ref[...] * (1.0 / math.sqrt(HEAD_DIM))).astype(BF16)
    qpos = i * T + lax.broadcasted_iota(jnp.int32, (T, T), 0)
    kofs = lax.broadcasted_iota(jnp.int32, (T, T), 1)
    r_i = lax.broadcasted_iota(jnp.int32, (T, T), 0)
    later_mat = (r_i > kofs).astype(BF16)

    def cond(carry):
        j, rest, _ = carry
        return jnp.logical_and(j >= 0, jnp.max(rest) > SB_DEAD_LOG)

    def body(carry):
        j, rest, acc = carry
        start = pl.multiple_of(j * T, T)
        k = k_ref[pl.ds(start, T), :].astype(BF16)
        v = v_ref[pl.ds(start, T), :].astype(BF16)
        z = lax.dot_general(q, k, (((1,), (1,)), ((), ())), preferred_element_type=F32)
        sp = jnp.maximum(z, 0.0) + jnp.log1p(jnp.exp(-jnp.abs(z)))
        kpos = j * T + kofs
        live = (kpos < qpos) & (kpos >= META_PAD)
        l1 = jnp.where(live, -sp, 0.0)
        l1_hi = l1.astype(BF16)
        l1_lo = (l1 - l1_hi.astype(F32)).astype(BF16)
        later = (jnp.dot(l1_hi, later_mat, preferred_element_type=F32)
                 + jnp.dot(l1_lo, later_mat, preferred_element_type=F32))
        log_a = jnp.where(live, z - sp, -jnp.inf) + later + rest
        acc = acc + jnp.dot(jnp.exp(log_a).astype(BF16), v, preferred_element_type=F32)
        rest = rest + jnp.sum(l1, axis=1, keepdims=True)
        return j - 1, rest, acc

    init = (i, jnp.zeros((T, 1), F32), jnp.zeros((T, HEAD_DIM), F32))
    _, _, acc = lax.while_loop(cond, body, init)
    out_ref[...] = acc.astype(out_ref.dtype)


def _stickbreak(u, *, batch, seq_len):
    nb = seq_len // BLOCK
    u3 = u.reshape(batch, seq_len, N_IN)
    qcol, kcol, vcol = (c // HEAD_DIM for c in (COL_SBQ, COL_SBK, COL_SBV))
    return pl.pallas_call(
        _sb_kernel,
        grid=(batch, SB_HEADS, nb),
        in_specs=[
            pl.BlockSpec((None, BLOCK, HEAD_DIM), lambda b, h, i: (b, i, qcol + h)),
            pl.BlockSpec((None, seq_len, HEAD_DIM), lambda b, h, i: (b, 0, kcol + h)),
            pl.BlockSpec((None, seq_len, HEAD_DIM), lambda b, h, i: (b, 0, vcol + h)),
        ],
        out_specs=pl.BlockSpec((None, BLOCK, HEAD_DIM), lambda b, h, i: (b, i, h)),
        out_shape=jax.ShapeDtypeStruct((batch, seq_len, SB_WIDTH), BF16),
        compiler_params=pltpu.CompilerParams(
            dimension_semantics=("arbitrary", "arbitrary", "arbitrary"),
            vmem_limit_bytes=V7X_VMEM_LIMIT),
        name="stickbreak",
    )(u3, u3, u3).reshape(batch * seq_len, SB_WIDTH)


def _out_proj_kernel(ml_ref, sb_ref, pool_ref, halo_ref, h_ref, wo_ref, pw_ref, ps_ref, g_ref,
                     out_ref, ext_ref, *, tm, seq_len):
    i = pl.program_id(0)
    base = lax.rem(i * tm, seq_len)
    pos = base + lax.broadcasted_iota(jnp.int32, (tm, 1), 0)
    validf = (pos >= META_PAD).astype(F32)

    ext_ref[0:POOL_HALO, :] = jnp.where(base == 0, 0.0, halo_ref[...])
    ext_ref[POOL_HALO:POOL_HALO + tm, :] = pool_ref[...] * validf

    pooled = []
    for g, w in enumerate(POOL_WINDOWS):
        lo, hi = g * POOL_CH, (g + 1) * POOL_CH
        uf = ext_ref[POOL_HALO:POOL_HALO + tm, lo:hi]
        s = uf
        for d in range(1, w):
            s = s + ext_ref[POOL_HALO - d:POOL_HALO - d + tm, lo:hi]
        cnt = jnp.clip(pos - (META_PAD - 1), 0, w).astype(F32)
        y = s / jnp.maximum(cnt, 1.0) - uf
        yg = jnp.dot(y.astype(BF16), pw_ref[g], preferred_element_type=F32) * ps_ref[g:g + 1, :]
        pooled.append(yg.astype(BF16))
    pool_out = jnp.concatenate(pooled, axis=1)

    y = (jnp.dot(ml_ref[...], wo_ref[0:ML_WIDTH, :], preferred_element_type=F32)
         + jnp.dot(sb_ref[...], wo_ref[ML_WIDTH:ML_WIDTH + SB_WIDTH, :], preferred_element_type=F32)
         + jnp.dot(pool_out, wo_ref[ML_WIDTH + SB_WIDTH:, :], preferred_element_type=F32))
    out_ref[...] = h_ref[...] + _rms_scale(y, g_ref[...])


def _out_proj(ml, sb, u, h, w_out, pool_w, pool_scale, g_post, *, seq_len, tm):
    m = h.shape[0]
    halo_blocks = tm // POOL_HALO
    return pl.pallas_call(
        functools.partial(_out_proj_kernel, tm=tm, seq_len=seq_len),
        grid=(m // tm,),
        in_specs=[
            pl.BlockSpec((tm, ML_WIDTH), lambda i: (i, 0)),
            pl.BlockSpec((tm, SB_WIDTH), lambda i: (i, 0)),
            pl.BlockSpec((tm, POOL_WIDTH), lambda i: (i, COL_POOL // POOL_WIDTH)),
            pl.BlockSpec((POOL_HALO, POOL_WIDTH),
                         lambda i: (jnp.maximum(i * halo_blocks - 1, 0), COL_POOL // POOL_WIDTH)),
            pl.BlockSpec((tm, D_MODEL), lambda i: (i, 0)),
            pl.BlockSpec((D_MODEL, D_MODEL), lambda i: (0, 0)),
            pl.BlockSpec((len(POOL_WINDOWS), POOL_CH, POOL_CH), lambda i: (0, 0, 0)),
            pl.BlockSpec((len(POOL_WINDOWS), POOL_CH), lambda i: (0, 0)),
            pl.BlockSpec((1, D_MODEL), lambda i: (0, 0)),
        ],
        out_specs=pl.BlockSpec((tm, D_MODEL), lambda i: (i, 0)),
        out_shape=jax.ShapeDtypeStruct((m, D_MODEL), F32),
        scratch_shapes=[pltpu.VMEM((POOL_HALO + tm, POOL_WIDTH), F32)],
        compiler_params=pltpu.CompilerParams(
            dimension_semantics=("arbitrary",),
            vmem_limit_bytes=V7X_VMEM_LIMIT),
        name="out_proj",
    )(ml, sb, u, u, h, w_out, pool_w, pool_scale, g_post)


def _ffn_kernel(h_ref, gpre_ref, wg_ref, wu_ref, wd_ref, gpost_ref, out_ref, xn_ref, acc_ref):
    f = pl.program_id(1)

    @pl.when(f == 0)
    def _():
        xn_ref[...] = _rms_scale(h_ref[...], gpre_ref[...]).astype(BF16)
        acc_ref[...] = jnp.zeros_like(acc_ref)

    xn = xn_ref[...]
    gate = jnp.dot(xn, wg_ref[...], preferred_element_type=F32)
    up = jnp.dot(xn, wu_ref[...], preferred_element_type=F32)
    act = (gate * jax.nn.sigmoid(gate) * up).astype(BF16)
    acc_ref[...] += jnp.dot(act, wd_ref[...], preferred_element_type=F32)

    @pl.when(f == pl.num_programs(1) - 1)
    def _():
        out_ref[...] = h_ref[...] + _rms_scale(acc_ref[...], gpost_ref[...])


def _ffn(h, g_pre, w_gate_up, w_down, g_post, *, tm, tf):
    m = h.shape[0]
    nf = FFN_HIDDEN // tf
    return pl.pallas_call(
        _ffn_kernel,
        grid=(m // tm, nf),
        in_specs=[
            pl.BlockSpec((tm, D_MODEL), lambda i, f: (i, 0)),
            pl.BlockSpec((1, D_MODEL), lambda i, f: (0, 0)),
            pl.BlockSpec((D_MODEL, tf), lambda i, f: (0, f)),
            pl.BlockSpec((D_MODEL, tf), lambda i, f: (0, nf + f)),
            pl.BlockSpec((tf, D_MODEL), lambda i, f: (f, 0)),
            pl.BlockSpec((1, D_MODEL), lambda i, f: (0, 0)),
        ],
        out_specs=pl.BlockSpec((tm, D_MODEL), lambda i, f: (i, 0)),
        out_shape=jax.ShapeDtypeStruct((m, D_MODEL), F32),
        scratch_shapes=[pltpu.VMEM((tm, D_MODEL), BF16), pltpu.VMEM((tm, D_MODEL), F32)],
        compiler_params=pltpu.CompilerParams(
            dimension_semantics=("arbitrary", "arbitrary"),
            vmem_limit_bytes=V7X_VMEM_LIMIT),
        name="ffn",
    )(h, g_pre, w_gate_up, w_gate_up, w_down, g_post)


def _row_tile(seq_len, target):
    best = 16
    for t in range(16, target + 1, 16):
        if seq_len % t == 0:
            best = t
    return best


def _reorder_w_in(w):
    offs = [0]
    for s in IN_SIZES:
        offs.append(offs[-1] + s)
    seg = lambda a: w[:, offs[a]:offs[a + 1]]
    pad = jnp.zeros((w.shape[0], GATE_W - 2 * ML_HEADS), w.dtype)
    cols = [seg(0), seg(1), seg(2), seg(3), seg(6), seg(7), seg(8), seg(9), seg(4), seg(5), pad]
    return jnp.concatenate(cols, axis=1).astype(BF16)


def kernel(x, meta_tokens, w_in, ml_conv_w, ml_igate_b, ml_fgate_b, pool_w, pool_scale, w_out,
           g_mix_pre, g_mix_post, g_ffn_pre, g_ffn_post, w_gate_up, w_down):
    batch, seq, d = x.shape
    depth = w_in.shape[0]
    seq_len = BLOCK + seq
    meta = jnp.broadcast_to(meta_tokens[None].astype(x.dtype), (batch, N_META, d))
    h = jnp.concatenate([jnp.zeros((batch, META_PAD, d), x.dtype), meta, x], axis=1)
    h = h.reshape(batch * seq_len, d)

    tm_mm = _row_tile(seq_len, 640)
    tm_out = _row_tile(seq_len, 320)
    for l in range(depth):
        gate_bias = jnp.concatenate(
            [ml_igate_b[l], ml_fgate_b[l], jnp.zeros((128 - 2 * ML_HEADS,), F32)])[None, :]
        u = _in_proj(h, g_mix_pre[l][None, :], _reorder_w_in(w_in[l]), seq_len=seq_len, tm=tm_mm)
        ml = _mlstm(u, ml_conv_w[l], gate_bias, batch=batch, seq_len=seq_len)
        sb = _stickbreak(u, batch=batch, seq_len=seq_len)
        h = _out_proj(ml, sb, u, h, w_out[l].astype(BF16), pool_w[l].astype(BF16), pool_scale[l],
                      g_mix_post[l][None, :], seq_len=seq_len, tm=tm_out)
        h = _ffn(h, g_ffn_pre[l][None, :], w_gate_up[l].astype(BF16), w_down[l].astype(BF16),
                 g_ffn_post[l][None, :], tm=tm_mm, tf=512)
    return h.reshape(batch, seq_len, d)[:, BLOCK:, :]
```

```python
import functools
import math

import jax
import jax.numpy as jnp
from jax import lax
from jax.experimental import pallas as pl
from jax.experimental.pallas import tpu as pltpu

F32 = jnp.float32
BF16 = jnp.bfloat16

D_MODEL = 2048
N_META = 16
BLOCK = 128
META_PAD = BLOCK - N_META
EPS = 1e-6

HEAD_DIM = 128
ML_HEADS = 6
ML_DQK = 64
ML_QK_WIDTH = ML_HEADS * ML_DQK
ML_WIDTH = ML_HEADS * HEAD_DIM
ML_CONV = 4
SB_HEADS = 6
SB_WIDTH = SB_HEADS * HEAD_DIM
POOL_WINDOWS = (2, 4, 8, 16)
POOL_CH = 128
POOL_WIDTH = POOL_CH * len(POOL_WINDOWS)
POOL_HALO = 16
FFN_HIDDEN = 5632
IN_SIZES = (ML_QK_WIDTH, ML_QK_WIDTH, ML_WIDTH, ML_WIDTH, ML_HEADS, ML_HEADS,
            SB_WIDTH, SB_WIDTH, SB_WIDTH, POOL_WIDTH)
IN_WIDTH = sum(IN_SIZES)
IN_GATE_OFF = 2 * ML_QK_WIDTH + 2 * ML_WIDTH
N_GATES = 2 * ML_HEADS

SEG = 768
COL_QK, COL_MLV, COL_MLO, COL_SBQ, COL_SBK, COL_SBV = (i * SEG for i in range(6))
COL_POOL = 6 * SEG
U_WIDTH = COL_POOL + POOL_WIDTH
GATE_W = 128
N_IN = U_WIDTH + GATE_W
IN_CHUNKS = tuple((i * SEG, SEG) for i in range(6)) + ((COL_POOL, POOL_WIDTH),)

V7X_VMEM_LIMIT = 56 * 1024 * 1024

SB_DEAD_LOG = -104.0


def _rms_scale(x, g):
    ms = jnp.mean(x * x, axis=-1, keepdims=True)
    return x * lax.rsqrt(ms + EPS) * g


def _w_in_prep_kernel(w_ref, out_ref):
    x = w_ref[...]
    out_ref[:, 0:IN_GATE_OFF] = x[:, 0:IN_GATE_OFF].astype(BF16)
    out_ref[:, IN_GATE_OFF:U_WIDTH] = x[:, IN_GATE_OFF + N_GATES:IN_WIDTH].astype(BF16)
    lane = lax.broadcasted_iota(jnp.int32, (1, GATE_W), 1)
    gates = jnp.where(lane < N_GATES, x[:, IN_GATE_OFF:IN_GATE_OFF + GATE_W], 0.0)
    out_ref[:, U_WIDTH:N_IN] = gates.astype(BF16)


def _w_in_prep(w_in, *, tr=256):
    depth = w_in.shape[0]
    return pl.pallas_call(
        _w_in_prep_kernel,
        grid=(depth, D_MODEL // tr),
        in_specs=[pl.BlockSpec((None, tr, IN_WIDTH), lambda l, r: (l, r, 0))],
        out_specs=pl.BlockSpec((None, tr, N_IN), lambda l, r: (l, r, 0)),
        out_shape=jax.ShapeDtypeStruct((depth, D_MODEL, N_IN), BF16),
        compiler_params=pltpu.CompilerParams(
            dimension_semantics=("arbitrary", "arbitrary"),
            vmem_limit_bytes=V7X_VMEM_LIMIT),
        name="w_in_prep",
    )(w_in)


def _in_proj_kernel(h_ref, g_ref, w_ref, u_ref, gate_ref, xn_ref, *, tm, seq_len):
    i = pl.program_id(0)
    xn_ref[...] = _rms_scale(h_ref[...], g_ref[...]).astype(BF16)
    pos = lax.rem(i * tm, seq_len) + lax.broadcasted_iota(jnp.int32, (tm, 1), 0)
    validf = (pos >= META_PAD).astype(F32)
    for c0, cw in IN_CHUNKS:
        acc = jnp.dot(xn_ref[...], w_ref[:, c0:c0 + cw], preferred_element_type=F32)
        u_ref[:, c0:c0 + cw] = (acc * validf).astype(BF16)
    acc = jnp.dot(xn_ref[...], w_ref[:, U_WIDTH:N_IN], preferred_element_type=F32)
    gate_ref[...] = acc * validf


def _in_proj(h, g, w_all, layer, *, seq_len, tm):
    m = h.shape[0]
    return pl.pallas_call(
        functools.partial(_in_proj_kernel, tm=tm, seq_len=seq_len),
        grid=(m // tm,),
        in_specs=[
            pl.BlockSpec((tm, D_MODEL), lambda i: (i, 0)),
            pl.BlockSpec((1, D_MODEL), lambda i: (0, 0)),
            pl.BlockSpec((None, D_MODEL, N_IN), lambda i: (layer, 0, 0), pipeline_mode=pl.Buffered(1)),
        ],
        out_specs=[
            pl.BlockSpec((tm, U_WIDTH), lambda i: (i, 0)),
            pl.BlockSpec((tm, GATE_W), lambda i: (i, 0)),
        ],
        out_shape=[jax.ShapeDtypeStruct((m, U_WIDTH), BF16), jax.ShapeDtypeStruct((m, GATE_W), F32)],
        scratch_shapes=[pltpu.VMEM((tm, D_MODEL), BF16)],
        compiler_params=pltpu.CompilerParams(
            dimension_semantics=("arbitrary",),
            vmem_limit_bytes=V7X_VMEM_LIMIT),
        name="in_proj",
    )(h, g, w_all)


def _log_sigmoid(x):
    return jnp.minimum(x, 0.0) - jnp.log(1.0 + jnp.exp(-jnp.abs(x)))


def _mlstm_kernel(qk_ref, v_ref, o_ref, gate_ref, convw_ref, gbias_ref, out_ref,
                  ext_ref, ct_ref, n_ref, m_ref):
    c = pl.program_id(1)
    T = BLOCK
    CONV_PAD = 8

    @pl.when(c == 0)
    def _():
        ext_ref[0:CONV_PAD, :] = jnp.zeros((CONV_PAD, 2 * ML_QK_WIDTH), F32)
        ct_ref[...] = jnp.zeros_like(ct_ref)
        n_ref[...] = jnp.zeros_like(n_ref)
        m_ref[...] = jnp.zeros_like(m_ref)

    ext_ref[CONV_PAD:CONV_PAD + T, :] = qk_ref[...].astype(F32)
    conv = jnp.zeros((T, 2 * ML_QK_WIDTH), F32)
    for j in range(ML_CONV):
        off = CONV_PAD - (ML_CONV - 1) + j
        conv = conv + convw_ref[j:j + 1, :] * ext_ref[off:off + T, :]
    ext_ref[0:CONV_PAD, :] = ext_ref[T:T + CONV_PAD, :]
    qk = conv * jax.nn.sigmoid(conv)
    q_all = qk[:, :ML_QK_WIDTH]
    k_all = qk[:, ML_QK_WIDTH:] * (1.0 / math.sqrt(ML_DQK))

    row = lax.broadcasted_iota(jnp.int32, (T, 1), 0)
    valid = (c * T + row) >= META_PAD
    gpre = gate_ref[...] + gbias_ref[...]
    log_i = jnp.where(valid, gpre, -jnp.inf)
    log_f = jnp.where(valid, _log_sigmoid(gpre), 0.0)
    r_i = lax.broadcasted_iota(jnp.int32, (T, T), 0)
    c_i = lax.broadcasted_iota(jnp.int32, (T, T), 1)
    causal = c_i <= r_i
    tri = causal.astype(F32)
    bcum = jnp.dot(tri, log_f, preferred_element_type=F32, precision=lax.Precision.HIGHEST)
    bcum_t = bcum.T
    log_i_t = log_i.T

    lane = lax.broadcasted_iota(jnp.int32, (1, T), 1)
    heads = range(ML_HEADS)
    pairs = range(ML_HEADS // 2)
    halves = [(lane >= ML_DQK * e) & (lane < ML_DQK * (e + 1)) for e in range(2)]
    k_pairs = [k_all[:, 128 * p:128 * (p + 1)] for p in pairs]
    k_bf = [k.astype(BF16) for k in k_pairs]
    kt_bf = [k.T.astype(BF16) for k in k_pairs]
    ct_old = [ct_ref[p] for p in pairs]
    n_old = [n_ref[p:p + 1, :] for p in pairs]
    qms = [jnp.where(halves[h % 2], q_all[:, 128 * (h // 2):128 * (h // 2 + 1)], 0.0).astype(BF16)
           for h in heads]
    vs = [v_ref[:, 128 * h:128 * (h + 1)] for h in heads]

    qk = [lax.dot_general(qms[h], k_bf[h // 2], (((1,), (1,)), ((), ())), preferred_element_type=F32)
          for h in heads]
    q_state = [jnp.dot(qms[h], ct_old[h // 2].astype(BF16), preferred_element_type=F32) for h in heads]

    s_mats, a_inters, m_ts, wvs, n_locs, a_olds, a_locs, m_news = [], [], [], [], [], [], [], []
    for h in heads:
        b_col = bcum[:, ML_HEADS + h:ML_HEADS + h + 1]
        ig_col = log_i[:, h:h + 1]
        a_row = log_i_t[h:h + 1, :] - bcum_t[ML_HEADS + h:ML_HEADS + h + 1, :]
        m_prev = m_ref[h:h + 1, 0:1]
        b_last = b_col[T - 1:T, :]
        dm = jnp.where(causal, b_col + a_row, -jnp.inf)
        inter = b_col + m_prev
        m_t = jnp.maximum(inter, jnp.max(dm, axis=1, keepdims=True))
        s_mats.append(qk[h] * jnp.exp(dm - m_t))
        a_inters.append(jnp.exp(inter - m_t))
        m_ts.append(m_t)
        g_col = b_last + ig_col - b_col
        m_loc = jnp.max(g_col, axis=0, keepdims=True)
        w_loc = jnp.exp(g_col - m_loc)
        m_new = jnp.maximum(b_last + m_prev, m_loc)
        a_olds.append(jnp.exp(b_last + m_prev - m_new))
        a_locs.append(jnp.exp(m_loc - m_new))
        m_news.append(m_new)
        wvs.append((w_loc * vs[h].astype(F32)).astype(BF16))
        n_locs.append(jnp.sum(w_loc * k_pairs[h // 2], axis=0, keepdims=True))

    intra = [jnp.dot(s_mats[h].astype(BF16), vs[h], preferred_element_type=F32) for h in heads]
    ct_loc = [jnp.dot(kt_bf[h // 2], wvs[h], preferred_element_type=F32) for h in heads]

    outs = []
    for h in heads:
        num = intra[h] + a_inters[h] * q_state[h]
        qn = jnp.sum(qms[h].astype(F32) * n_old[h // 2], axis=1, keepdims=True)
        den = jnp.sum(s_mats[h], axis=1, keepdims=True) + a_inters[h] * qn
        hm = num / jnp.maximum(jnp.abs(den), jnp.exp(-m_ts[h]))
        outs.append(hm * jax.nn.sigmoid(o_ref[:, 128 * h:128 * (h + 1)].astype(F32)))
    out_ref[...] = jnp.concatenate(outs, axis=1).astype(out_ref.dtype)

    for h in heads:
        p, e = divmod(h, 2)
        lo, hi = ML_DQK * e, ML_DQK * (e + 1)
        ct_ref[p, lo:hi, :] = a_olds[h] * ct_old[p][lo:hi, :] + a_locs[h] * ct_loc[h][lo:hi, :]
        m_ref[h:h + 1, :] = jnp.broadcast_to(m_news[h], (1, T))
    for p in pairs:
        h0, h1 = 2 * p, 2 * p + 1
        n_ref[p:p + 1, :] = jnp.where(halves[0], a_olds[h0] * n_old[p] + a_locs[h0] * n_locs[h0],
                                      a_olds[h1] * n_old[p] + a_locs[h1] * n_locs[h1])


def _mlstm(u, gates, conv_w, gate_bias, *, batch, seq_len):
    nc = seq_len // BLOCK
    row = lambda b, c: b * nc + c
    return pl.pallas_call(
        _mlstm_kernel,
        grid=(batch, nc),
        in_specs=[
            pl.BlockSpec((BLOCK, SEG), lambda b, c: (row(b, c), COL_QK // SEG)),
            pl.BlockSpec((BLOCK, SEG), lambda b, c: (row(b, c), COL_MLV // SEG)),
            pl.BlockSpec((BLOCK, SEG), lambda b, c: (row(b, c), COL_MLO // SEG)),
            pl.BlockSpec((BLOCK, GATE_W), lambda b, c: (row(b, c), 0)),
            pl.BlockSpec((ML_CONV, 2 * ML_QK_WIDTH), lambda b, c: (0, 0)),
            pl.BlockSpec((1, GATE_W), lambda b, c: (0, 0)),
        ],
        out_specs=pl.BlockSpec((BLOCK, ML_WIDTH), lambda b, c: (row(b, c), 0)),
        out_shape=jax.ShapeDtypeStruct((batch * seq_len, ML_WIDTH), BF16),
        scratch_shapes=[
            pltpu.VMEM((8 + BLOCK, 2 * ML_QK_WIDTH), F32),
            pltpu.VMEM((ML_HEADS // 2, 128, 128), F32),
            pltpu.VMEM((8, 128), F32),
            pltpu.VMEM((8, 128), F32),
        ],
        compiler_params=pltpu.CompilerParams(
            dimension_semantics=("arbitrary", "arbitrary"),
            vmem_limit_bytes=V7X_VMEM_LIMIT),
        name="mlstm",
    )(u, u, u, gates, conv_w, gate_bias)


def _sb_kernel(q_ref, k_ref, v_ref, out_ref, acc_ref, rest_ref):
    i = pl.program_id(1)
    T = BLOCK
    scale = 1.0 / math.sqrt(HEAD_DIM)
    acc_ref[...] = jnp.zeros_like(acc_ref)
    rest_ref[...] = jnp.zeros_like(rest_ref)
    qpos = i * T + lax.broadcasted_iota(jnp.int32, (T, T), 0)
    kofs = lax.broadcasted_iota(jnp.int32, (T, T), 1)
    r_w = lax.broadcasted_iota(jnp.int32, (T, 2 * T), 0)
    c_w = lax.broadcasted_iota(jnp.int32, (T, 2 * T), 1)
    later_ones = ((r_w > c_w) | (c_w >= T)).astype(BF16)

    def cond(carry):
        j, worst = carry
        return jnp.logical_and(j >= 0, worst > SB_DEAD_LOG)

    def body(carry):
        j, _ = carry
        start = pl.multiple_of(j * T, T)
        kpos = j * T + kofs
        live = (kpos < qpos) & (kpos >= META_PAD)
        heads = range(SB_HEADS)
        cols = [slice(HEAD_DIM * h, HEAD_DIM * (h + 1)) for h in heads]
        zs = [lax.dot_general(q_ref[:, cols[h]], k_ref[pl.ds(start, T), cols[h]], (((1,), (1,)), ((), ())),
                              preferred_element_type=F32) * scale for h in heads]
        his, los, log_betas = [], [], []
        for h in heads:
            z = zs[h]
            sp = jnp.maximum(z, 0.0) + jnp.log(1.0 + jnp.exp(-jnp.abs(z)))
            l1 = jnp.where(live, -sp, 0.0)
            l1_hi = l1.astype(BF16)
            his.append(l1_hi)
            los.append((l1 - l1_hi.astype(F32)).astype(BF16))
            log_betas.append(jnp.where(live, z - sp, -jnp.inf))
        sums = [jnp.dot(his[h], later_ones, preferred_element_type=F32)
                + jnp.dot(los[h], later_ones, preferred_element_type=F32) for h in heads]
        probs = []
        worst = None
        for h in heads:
            rest = rest_ref[h]
            probs.append(jnp.exp(log_betas[h] + sums[h][:, :T] + rest).astype(BF16))
            rest = rest + sums[h][:, T:]
            rest_ref[h] = rest
            worst = rest if worst is None else jnp.maximum(worst, rest)
        for h in heads:
            acc_ref[:, cols[h]] += jnp.dot(probs[h], v_ref[pl.ds(start, T), cols[h]],
                                           preferred_element_type=F32)
        return j - 1, jnp.max(worst)

    lax.while_loop(cond, body, (i, jnp.float32(0.0)))
    out_ref[...] = acc_ref[...].astype(out_ref.dtype)


def _stickbreak(u, *, batch, seq_len):
    nb = seq_len // BLOCK
    u3 = u.reshape(batch, seq_len, U_WIDTH)
    whole = functools.partial(pl.BlockSpec, (None, seq_len, SB_WIDTH), pipeline_mode=pl.Buffered(1))
    return pl.pallas_call(
        _sb_kernel,
        grid=(batch, nb),
        in_specs=[
            pl.BlockSpec((None, BLOCK, SB_WIDTH), lambda b, i: (b, i, COL_SBQ // SEG)),
            whole(lambda b, i: (b, 0, COL_SBK // SEG)),
            whole(lambda b, i: (b, 0, COL_SBV // SEG)),
        ],
        out_specs=pl.BlockSpec((None, BLOCK, SB_WIDTH), lambda b, i: (b, i, 0)),
        out_shape=jax.ShapeDtypeStruct((batch, seq_len, SB_WIDTH), BF16),
        scratch_shapes=[
            pltpu.VMEM((BLOCK, SB_WIDTH), F32),
            pltpu.VMEM((SB_HEADS, BLOCK, BLOCK), F32),
        ],
        compiler_params=pltpu.CompilerParams(
            dimension_semantics=("arbitrary", "arbitrary"),
            vmem_limit_bytes=V7X_VMEM_LIMIT),
        name="stickbreak",
    )(u3, u3, u3).reshape(batch * seq_len, SB_WIDTH)


def _out_proj_kernel(ml_ref, sb_ref, pool_ref, halo_ref, h_ref, wo_ref, pw_ref, ps_ref, g_ref,
                     out_ref, ext_ref, *, tm, seq_len):
    i = pl.program_id(0)
    base = lax.rem(i * tm, seq_len)
    pos = base + lax.broadcasted_iota(jnp.int32, (tm, 1), 0)
    validf = (pos >= META_PAD).astype(F32)

    ext_ref[0:POOL_HALO, :] = jnp.where(base == 0, 0.0, halo_ref[...].astype(F32))
    ext_ref[POOL_HALO:POOL_HALO + tm, :] = pool_ref[...].astype(F32) * validf

    pooled = []
    for g, w in enumerate(POOL_WINDOWS):
        lo, hi = g * POOL_CH, (g + 1) * POOL_CH
        uf = ext_ref[POOL_HALO:POOL_HALO + tm, lo:hi]
        s = uf
        for d in range(1, w):
            s = s + ext_ref[POOL_HALO - d:POOL_HALO - d + tm, lo:hi]
        cnt = jnp.clip(pos - (META_PAD - 1), 0, w).astype(F32)
        y = s / jnp.maximum(cnt, 1.0) - uf
        yg = jnp.dot(y.astype(BF16), pw_ref[g], preferred_element_type=F32) * ps_ref[g:g + 1, :]
        pooled.append(yg.astype(BF16))
    pool_out = jnp.concatenate(pooled, axis=1)

    y = (jnp.dot(ml_ref[...], wo_ref[0:ML_WIDTH, :], preferred_element_type=F32)
         + jnp.dot(sb_ref[...], wo_ref[ML_WIDTH:ML_WIDTH + SB_WIDTH, :], preferred_element_type=F32)
         + jnp.dot(pool_out, wo_ref[ML_WIDTH + SB_WIDTH:, :], preferred_element_type=F32))
    out_ref[...] = h_ref[...] + _rms_scale(y, g_ref[...])


def _out_proj(ml, sb, u, h, w_out, pool_w, pool_scale, g_post, *, seq_len, tm):
    m = h.shape[0]
    halo_blocks = tm // POOL_HALO
    return pl.pallas_call(
        functools.partial(_out_proj_kernel, tm=tm, seq_len=seq_len),
        grid=(m // tm,),
        in_specs=[
            pl.BlockSpec((tm, ML_WIDTH), lambda i: (i, 0)),
            pl.BlockSpec((tm, SB_WIDTH), lambda i: (i, 0)),
            pl.BlockSpec((tm, POOL_WIDTH), lambda i: (i, COL_POOL // POOL_WIDTH)),
            pl.BlockSpec((POOL_HALO, POOL_WIDTH),
                         lambda i: (jnp.maximum(i * halo_blocks - 1, 0), COL_POOL // POOL_WIDTH)),
            pl.BlockSpec((tm, D_MODEL), lambda i: (i, 0)),
            pl.BlockSpec((D_MODEL, D_MODEL), lambda i: (0, 0), pipeline_mode=pl.Buffered(1)),
            pl.BlockSpec((len(POOL_WINDOWS), POOL_CH, POOL_CH), lambda i: (0, 0, 0)),
            pl.BlockSpec((len(POOL_WINDOWS), POOL_CH), lambda i: (0, 0)),
            pl.BlockSpec((1, D_MODEL), lambda i: (0, 0)),
        ],
        out_specs=pl.BlockSpec((tm, D_MODEL), lambda i: (i, 0)),
        out_shape=jax.ShapeDtypeStruct((m, D_MODEL), F32),
        scratch_shapes=[pltpu.VMEM((POOL_HALO + tm, POOL_WIDTH), F32)],
        compiler_params=pltpu.CompilerParams(
            dimension_semantics=("arbitrary",),
            vmem_limit_bytes=V7X_VMEM_LIMIT),
        name="out_proj",
    )(ml, sb, u, u, h, w_out, pool_w, pool_scale, g_post)


def _ffn_kernel(h_ref, gpre_ref, wg_ref, wu_ref, wd_ref, gpost_ref, out_ref, xn_ref, acc_ref):
    f = pl.program_id(1)

    @pl.when(f == 0)
    def _():
        xn_ref[...] = _rms_scale(h_ref[...], gpre_ref[...]).astype(BF16)
        acc_ref[...] = jnp.zeros_like(acc_ref)

    xn = xn_ref[...]
    gate = jnp.dot(xn, wg_ref[...], preferred_element_type=F32)
    up = jnp.dot(xn, wu_ref[...], preferred_element_type=F32)
    act = (gate * jax.nn.sigmoid(gate) * up).astype(BF16)
    acc_ref[...] += jnp.dot(act, wd_ref[...], preferred_element_type=F32)

    @pl.when(f == pl.num_programs(1) - 1)
    def _():
        out_ref[...] = h_ref[...] + _rms_scale(acc_ref[...], gpost_ref[...])


def _ffn(h, g_pre, w_gate_up, w_down, g_post, *, tm, tf):
    m = h.shape[0]
    nf = FFN_HIDDEN // tf
    return pl.pallas_call(
        _ffn_kernel,
        grid=(m // tm, nf),
        in_specs=[
            pl.BlockSpec((tm, D_MODEL), lambda i, f: (i, 0)),
            pl.BlockSpec((1, D_MODEL), lambda i, f: (0, 0)),
            pl.BlockSpec((D_MODEL, tf), lambda i, f: (0, f)),
            pl.BlockSpec((D_MODEL, tf), lambda i, f: (0, nf + f)),
            pl.BlockSpec((tf, D_MODEL), lambda i, f: (f, 0)),
            pl.BlockSpec((1, D_MODEL), lambda i, f: (0, 0)),
        ],
        out_specs=pl.BlockSpec((tm, D_MODEL), lambda i, f: (i, 0)),
        out_shape=jax.ShapeDtypeStruct((m, D_MODEL), F32),
        scratch_shapes=[pltpu.VMEM((tm, D_MODEL), BF16), pltpu.VMEM((tm, D_MODEL), F32)],
        compiler_params=pltpu.CompilerParams(
            dimension_semantics=("arbitrary", "arbitrary"),
            vmem_limit_bytes=V7X_VMEM_LIMIT),
        name="ffn",
    )(h, g_pre, w_gate_up, w_gate_up, w_down, g_post)


def _row_tile(seq_len, target):
    best = 16
    for t in range(16, target + 1, 16):
        if seq_len % t == 0:
            best = t
    return best


def kernel(x, meta_tokens, w_in, ml_conv_w, ml_igate_b, ml_fgate_b, pool_w, pool_scale, w_out,
           g_mix_pre, g_mix_post, g_ffn_pre, g_ffn_post, w_gate_up, w_down):
    batch, seq, d = x.shape
    depth = w_in.shape[0]
    seq_len = BLOCK + seq
    meta = jnp.broadcast_to(meta_tokens[None].astype(x.dtype), (batch, N_META, d))
    h = jnp.concatenate([jnp.zeros((batch, META_PAD, d), x.dtype), meta, x], axis=1)
    h = h.reshape(batch * seq_len, d)

    w_in_r = _w_in_prep(w_in)
    tm_mm = _row_tile(seq_len, 640)
    tm_out = _row_tile(seq_len, 640)
    for l in range(depth):
        gate_bias = jnp.concatenate(
            [ml_igate_b[l], ml_fgate_b[l], jnp.zeros((GATE_W - N_GATES,), F32)])[None, :]
        u, gates = _in_proj(h, g_mix_pre[l][None, :], w_in_r, l, seq_len=seq_len, tm=tm_mm)
        ml = _mlstm(u, gates, ml_conv_w[l], gate_bias, batch=batch, seq_len=seq_len)
        sb = _stickbreak(u, batch=batch, seq_len=seq_len)
        h = _out_proj(ml, sb, u, h, w_out[l].astype(BF16), pool_w[l].astype(BF16), pool_scale[l],
                      g_mix_post[l][None, :], seq_len=seq_len, tm=tm_out)
        h = _ffn(h, g_ffn_pre[l][None, :], w_gate_up[l].astype(BF16), w_down[l].astype(BF16),
                 g_ffn_post[l][None, :], tm=tm_mm, tf=512)
    return h.reshape(batch, seq_len, d)[:, BLOCK:, :]
```

```python
import functools
import math

import jax
import jax.numpy as jnp
from jax import lax
from jax.experimental import pallas as pl
from jax.experimental.pallas import tpu as pltpu

F32 = jnp.float32
BF16 = jnp.bfloat16

D_MODEL = 2048
N_META = 16
BLOCK = 128
META_PAD = BLOCK - N_META
EPS = 1e-6

HEAD_DIM = 128
ML_HEADS = 6
ML_DQK = 64
ML_QK_WIDTH = ML_HEADS * ML_DQK
ML_WIDTH = ML_HEADS * HEAD_DIM
ML_CONV = 4
SB_HEADS = 6
SB_WIDTH = SB_HEADS * HEAD_DIM
POOL_WINDOWS = (2, 4, 8, 16)
POOL_CH = 128
POOL_WIDTH = POOL_CH * len(POOL_WINDOWS)
POOL_HALO = 16
FFN_HIDDEN = 5632
IN_SIZES = (ML_QK_WIDTH, ML_QK_WIDTH, ML_WIDTH, ML_WIDTH, ML_HEADS, ML_HEADS,
            SB_WIDTH, SB_WIDTH, SB_WIDTH, POOL_WIDTH)
IN_WIDTH = sum(IN_SIZES)
IN_GATE_OFF = 2 * ML_QK_WIDTH + 2 * ML_WIDTH
N_GATES = 2 * ML_HEADS

SEG = 768
COL_QK, COL_MLV, COL_MLO, COL_SBQ, COL_SBK, COL_SBV = (i * SEG for i in range(6))
COL_POOL = 6 * SEG
U_WIDTH = COL_POOL + POOL_WIDTH
GATE_W = 128
N_IN = U_WIDTH + GATE_W
IN_CHUNKS = tuple((i * SEG, SEG) for i in range(6)) + ((COL_POOL, POOL_WIDTH),)

V7X_VMEM_LIMIT = 56 * 1024 * 1024

SB_DEAD_LOG = -104.0
SB_TOP_ROWS = 32


def _rms_scale(x, g):
    ms = jnp.mean(x * x, axis=-1, keepdims=True)
    return x * lax.rsqrt(ms + EPS) * g


def _w_in_prep_kernel(w_ref, out_ref):
    x = w_ref[...]
    out_ref[:, 0:IN_GATE_OFF] = x[:, 0:IN_GATE_OFF].astype(BF16)
    out_ref[:, IN_GATE_OFF:U_WIDTH] = x[:, IN_GATE_OFF + N_GATES:IN_WIDTH].astype(BF16)
    lane = lax.broadcasted_iota(jnp.int32, (1, GATE_W), 1)
    gates = jnp.where(lane < N_GATES, x[:, IN_GATE_OFF:IN_GATE_OFF + GATE_W], 0.0)
    out_ref[:, U_WIDTH:N_IN] = gates.astype(BF16)


def _w_in_prep(w_in, *, tr=256):
    depth = w_in.shape[0]
    return pl.pallas_call(
        _w_in_prep_kernel,
        grid=(depth, D_MODEL // tr),
        in_specs=[pl.BlockSpec((None, tr, IN_WIDTH), lambda l, r: (l, r, 0))],
        out_specs=pl.BlockSpec((None, tr, N_IN), lambda l, r: (l, r, 0)),
        out_shape=jax.ShapeDtypeStruct((depth, D_MODEL, N_IN), BF16),
        compiler_params=pltpu.CompilerParams(
            dimension_semantics=("arbitrary", "arbitrary"),
            vmem_limit_bytes=V7X_VMEM_LIMIT),
        name="w_in_prep",
    )(w_in)


def _in_proj_kernel(h_ref, g_ref, w_ref, u_ref, gate_ref, xn_ref, *, tm, seq_len):
    i = pl.program_id(0)
    xn_ref[...] = _rms_scale(h_ref[...], g_ref[...]).astype(BF16)
    pos = lax.rem(i * tm, seq_len) + lax.broadcasted_iota(jnp.int32, (tm, 1), 0)
    validf = (pos >= META_PAD).astype(F32)
    for c0, cw in IN_CHUNKS:
        acc = jnp.dot(xn_ref[...], w_ref[:, c0:c0 + cw], preferred_element_type=F32)
        u_ref[:, c0:c0 + cw] = (acc * validf).astype(BF16)
    acc = jnp.dot(xn_ref[...], w_ref[:, U_WIDTH:N_IN], preferred_element_type=F32)
    gate_ref[...] = acc * validf


def _in_proj(h, g, w_all, layer, *, seq_len, tm):
    m = h.shape[0]
    return pl.pallas_call(
        functools.partial(_in_proj_kernel, tm=tm, seq_len=seq_len),
        grid=(m // tm,),
        in_specs=[
            pl.BlockSpec((tm, D_MODEL), lambda i: (i, 0)),
            pl.BlockSpec((1, D_MODEL), lambda i: (0, 0)),
            pl.BlockSpec((None, D_MODEL, N_IN), lambda i: (layer, 0, 0), pipeline_mode=pl.Buffered(1)),
        ],
        out_specs=[
            pl.BlockSpec((tm, U_WIDTH), lambda i: (i, 0)),
            pl.BlockSpec((tm, GATE_W), lambda i: (i, 0)),
        ],
        out_shape=[jax.ShapeDtypeStruct((m, U_WIDTH), BF16), jax.ShapeDtypeStruct((m, GATE_W), F32)],
        scratch_shapes=[pltpu.VMEM((tm, D_MODEL), BF16)],
        compiler_params=pltpu.CompilerParams(
            dimension_semantics=("arbitrary",),
            vmem_limit_bytes=V7X_VMEM_LIMIT),
        name="in_proj",
    )(h, g, w_all)


def _log_sigmoid(x):
    return jnp.minimum(x, 0.0) - jnp.log(1.0 + jnp.exp(-jnp.abs(x)))


def _split3_bf16(x):
    hi = x.astype(BF16)
    r1 = x - hi.astype(F32)
    mid = r1.astype(BF16)
    lo = (r1 - mid.astype(F32)).astype(BF16)
    return hi, mid, lo


def _mlstm_kernel(qkprev_ref, qk_ref, v_ref, o_ref, gate_ref, convw_ref, gbias_ref, out_ref,
                  ctn_ref, m_ref):
    c = pl.program_id(1)
    T = BLOCK
    NH = ML_HEADS

    @pl.when(c == 0)
    def _():
        ctn_ref[...] = jnp.zeros_like(ctn_ref)
        m_ref[...] = jnp.zeros_like(m_ref)

    cur = qk_ref[...]
    prev = jnp.where(c == 0, jnp.zeros_like(cur), qkprev_ref[...])
    both = jnp.concatenate([prev, cur], axis=0)
    rr = lax.broadcasted_iota(jnp.int32, ((ML_CONV - 1) * T, 2 * T), 0)
    cc = lax.broadcasted_iota(jnp.int32, ((ML_CONV - 1) * T, 2 * T), 1)
    delay = (ML_CONV - 1) - rr // T
    shift = (cc == T + rr % T - delay).astype(BF16)
    delayed = jnp.dot(shift, both, preferred_element_type=F32)
    conv = convw_ref[ML_CONV - 1:ML_CONV, :] * cur.astype(F32)
    for j in range(ML_CONV - 1):
        conv = conv + convw_ref[j:j + 1, :] * delayed[j * T:(j + 1) * T, :]
    qk = conv * jax.nn.sigmoid(conv)
    q_all = qk[:, :ML_QK_WIDTH]
    k_all = qk[:, ML_QK_WIDTH:] * (1.0 / math.sqrt(ML_DQK))

    row = lax.broadcasted_iota(jnp.int32, (T, 1), 0)
    valid = (c * T + row) >= META_PAD
    gpre = gate_ref[...] + gbias_ref[...]
    log_i = jnp.where(valid, gpre, -jnp.inf)
    log_f = jnp.where(valid, _log_sigmoid(gpre), 0.0)
    r_i = lax.broadcasted_iota(jnp.int32, (T, T), 0)
    c_i = lax.broadcasted_iota(jnp.int32, (T, T), 1)
    causal = c_i <= r_i
    tri = causal.astype(F32)
    bcum = jnp.dot(tri, log_f, preferred_element_type=F32, precision=lax.Precision.HIGHEST)
    bcum_t = bcum.T
    log_i_t = log_i.T
    e_r = lax.broadcasted_iota(jnp.int32, (T, 2 * NH * T), 0)
    e_c = lax.broadcasted_iota(jnp.int32, (T, 2 * NH * T), 1)
    spread = (e_r == e_c // T).astype(BF16)
    lane_g = lax.broadcasted_iota(jnp.int32, (1, T), 1)
    packed = jnp.where(lane_g < NH, gpre, bcum)
    on_lanes = sum(jnp.dot(part, spread, preferred_element_type=F32) for part in _split3_bf16(packed))

    lane = lax.broadcasted_iota(jnp.int32, (1, T), 1)
    heads = range(ML_HEADS)
    pairs = range(ML_HEADS // 2)
    halves = [(lane >= ML_DQK * e) & (lane < ML_DQK * (e + 1)) for e in range(2)]
    k_pairs = [k_all[:, 128 * p:128 * (p + 1)] for p in pairs]
    k_bf = [k.astype(BF16) for k in k_pairs]
    kt_bf = [k.T.astype(BF16) for k in k_pairs]
    ctn_old = [ctn_ref[p] for p in pairs]
    ctn_bf = [s.astype(BF16) for s in ctn_old]
    qms = [jnp.where(halves[h % 2], q_all[:, 128 * (h // 2):128 * (h // 2 + 1)], 0.0).astype(BF16)
           for h in heads]
    vs = [v_ref[:, 128 * h:128 * (h + 1)] for h in heads]
    ones_bf = jnp.ones((T, T), BF16)

    qk = [lax.dot_general(qms[h], k_bf[h // 2], (((1,), (1,)), ((), ())), preferred_element_type=F32)
          for h in heads]
    q_state = [jnp.dot(qms[h], ctn_bf[h // 2], preferred_element_type=F32) for h in heads]

    s_mats, a_inters, floor_dens, wvs, a_olds, a_locs, m_news = [], [], [], [], [], [], []
    for h in heads:
        b_t = on_lanes[:, (NH + h) * T:(NH + h + 1) * T]
        ig_t = jnp.where(valid, on_lanes[:, h * T:(h + 1) * T], -jnp.inf)
        a_row = log_i_t[h:h + 1, :] - bcum_t[NH + h:NH + h + 1, :]
        m_prev = m_ref[h:h + 1, :]
        b_last = b_t[T - 1:T, :]
        dm = jnp.where(causal, b_t + a_row, -jnp.inf)
        inter = b_t + m_prev
        m_t = jnp.maximum(inter, jnp.max(dm, axis=1, keepdims=True))
        s_mats.append(qk[h] * jnp.exp(dm - m_t))
        a_inters.append(jnp.exp(inter - m_t))
        floor_dens.append(jnp.exp(-m_t))
        g_t = b_last + ig_t - b_t
        m_loc = jnp.max(g_t, axis=0, keepdims=True)
        w_t = jnp.exp(g_t - m_loc)
        m_new = jnp.maximum(b_last + m_prev, m_loc)
        a_olds.append(jnp.exp(b_last + m_prev - m_new))
        a_locs.append(jnp.exp(m_loc - m_new))
        m_news.append(m_new)
        wvs.append(jnp.concatenate([(w_t * vs[h].astype(F32)).astype(BF16), w_t.astype(BF16)], axis=1))

    intra = [jnp.dot(s_mats[h].astype(BF16), jnp.concatenate([vs[h], ones_bf], axis=1),
                     preferred_element_type=F32) for h in heads]
    ctn_loc = [jnp.dot(kt_bf[h // 2], wvs[h], preferred_element_type=F32) for h in heads]

    outs = []
    for h in heads:
        num = intra[h][:, :T] + a_inters[h] * q_state[h][:, :T]
        den = intra[h][:, T:] + a_inters[h] * q_state[h][:, T:]
        hm = num / jnp.maximum(jnp.abs(den), floor_dens[h])
        outs.append(hm * jax.nn.sigmoid(o_ref[:, 128 * h:128 * (h + 1)].astype(F32)))
    out_ref[...] = jnp.concatenate(outs, axis=1).astype(out_ref.dtype)

    for h in heads:
        p, e = divmod(h, 2)
        lo, hi = ML_DQK * e, ML_DQK * (e + 1)
        for half in (slice(0, T), slice(T, 2 * T)):
            ctn_ref[p, lo:hi, half] = (a_olds[h] * ctn_old[p][lo:hi, half]
                                       + a_locs[h] * ctn_loc[h][lo:hi, half])
        m_ref[h:h + 1, :] = m_news[h]


def _mlstm(u, gates, conv_w, gate_bias, *, batch, seq_len):
    nc = seq_len // BLOCK
    row = lambda b, c: b * nc + c
    return pl.pallas_call(
        _mlstm_kernel,
        grid=(batch, nc),
        in_specs=[
            pl.BlockSpec((BLOCK, SEG), lambda b, c: (jnp.maximum(row(b, c) - 1, 0), COL_QK // SEG)),
            pl.BlockSpec((BLOCK, SEG), lambda b, c: (row(b, c), COL_QK // SEG)),
            pl.BlockSpec((BLOCK, SEG), lambda b, c: (row(b, c), COL_MLV // SEG)),
            pl.BlockSpec((BLOCK, SEG), lambda b, c: (row(b, c), COL_MLO // SEG)),
            pl.BlockSpec((BLOCK, GATE_W), lambda b, c: (row(b, c), 0)),
            pl.BlockSpec((ML_CONV, 2 * ML_QK_WIDTH), lambda b, c: (0, 0)),
            pl.BlockSpec((1, GATE_W), lambda b, c: (0, 0)),
        ],
        out_specs=pl.BlockSpec((BLOCK, ML_WIDTH), lambda b, c: (row(b, c), 0)),
        out_shape=jax.ShapeDtypeStruct((batch * seq_len, ML_WIDTH), BF16),
        scratch_shapes=[
            pltpu.VMEM((ML_HEADS // 2, 128, 256), F32),
            pltpu.VMEM((8, 128), F32),
        ],
        compiler_params=pltpu.CompilerParams(
            dimension_semantics=("arbitrary", "arbitrary"),
            vmem_limit_bytes=V7X_VMEM_LIMIT),
        name="mlstm",
    )(u, u, u, u, gates, conv_w, gate_bias)


def _sb_tiles(q_ref, k_ref, v_ref, acc_ref, rest_ref, later_ones, tiles, *, assign):
    T = BLOCK
    scale = 1.0 / math.sqrt(HEAD_DIM)
    heads = range(SB_HEADS)
    cols = [slice(HEAD_DIM * h, HEAD_DIM * (h + 1)) for h in heads]
    kofs = lax.broadcasted_iota(jnp.int32, (1, T), 1)
    nt = len(tiles)

    zs = {}
    for t, (start, _, (r0, r1)) in enumerate(tiles):
        for h in heads:
            zs[t, h] = lax.dot_general(q_ref[r0:r1, cols[h]], k_ref[pl.ds(start, T), cols[h]],
                                       (((1,), (1,)), ((), ())), preferred_element_type=F32) * scale
    his, los, log_betas = {}, {}, {}
    for t, (start, is_diag, (r0, r1)) in enumerate(tiles):
        live = (start + kofs) >= META_PAD
        if is_diag:
            live = live & (kofs < r0 + lax.broadcasted_iota(jnp.int32, (r1 - r0, T), 0))
        for h in heads:
            z = zs[t, h]
            sp = jnp.maximum(z, 0.0) + jnp.log(1.0 + jnp.exp(-jnp.abs(z)))
            l1 = jnp.where(live, -sp, 0.0)
            his[t, h] = l1.astype(BF16)
            los[t, h] = (l1 - his[t, h].astype(F32)).astype(BF16)
            log_betas[t, h] = jnp.where(live, z - sp, -jnp.inf)
    sums = {key: jnp.dot(his[key], later_ones, preferred_element_type=F32)
            + jnp.dot(los[key], later_ones, preferred_element_type=F32) for key in his}
    probs = {}
    for h in heads:
        rows0 = tiles[0][2]
        rest = None if assign else rest_ref[h, rows0[0]:rows0[1], :]
        for t, (_, _, (r0, r1)) in enumerate(tiles):
            lo, hi = r0 - rows0[0], r1 - rows0[0]
            log_a = log_betas[t, h] + sums[t, h][:, :T]
            new = sums[t, h][:, T:]
            if rest is not None:
                log_a = log_a + rest[lo:hi]
                new = new + rest[lo:hi]
                pieces = ([rest[:lo]] if lo else []) + [new] + ([rest[hi:]] if hi < rest.shape[0] else [])
                new = pieces[0] if len(pieces) == 1 else jnp.concatenate(pieces, axis=0)
            probs[t, h] = jnp.exp(log_a).astype(BF16)
            rest = new
        rest_ref[h, rows0[0]:rows0[1], :] = rest
    for t, (start, _, (r0, r1)) in enumerate(tiles):
        for h in heads:
            pv = jnp.dot(probs[t, h], v_ref[pl.ds(start, T), cols[h]], preferred_element_type=F32)
            if assign and t == 0:
                acc_ref[r0:r1, cols[h]] = pv
            else:
                acc_ref[r0:r1, cols[h]] += pv


def _sb_kernel(q_ref, k_ref, v_ref, out_ref, acc_ref, rest_ref):
    i = pl.program_id(1)
    T = BLOCK
    r_w = lax.broadcasted_iota(jnp.int32, (T, 2 * T), 0)
    c_w = lax.broadcasted_iota(jnp.int32, (T, 2 * T), 1)
    later_ones = ((r_w > c_w) | (c_w >= T)).astype(BF16)
    add_tiles = functools.partial(_sb_tiles, q_ref, k_ref, v_ref, acc_ref, rest_ref, later_ones)
    full, top, bottom = (0, T), (0, SB_TOP_ROWS), (SB_TOP_ROWS, T)
    start = lambda back: pl.multiple_of((i - back) * T, T)

    @pl.when(i == 0)
    def _():
        add_tiles([(start(0), True, full)], assign=True)

    @pl.when(i == 1)
    def _():
        add_tiles([(start(0), True, full), (start(1), False, full)], assign=True)

    @pl.when(i >= 2)
    def _():
        add_tiles([(start(0), True, full), (start(1), False, full), (start(2), False, top)], assign=True)

    def sweep(rows, j0):
        def alive():
            worst = rest_ref[0, rows[0]:rows[1], :]
            for h in range(1, SB_HEADS):
                worst = jnp.maximum(worst, rest_ref[h, rows[0]:rows[1], :])
            return jnp.max(worst)

        def cond(carry):
            j, worst = carry
            return jnp.logical_and(j >= 0, worst > SB_DEAD_LOG)

        def body(carry):
            j, _ = carry
            add_tiles([(pl.multiple_of(j * T, T), False, rows)], assign=False)
            return j - 1, alive()

        lax.while_loop(cond, body, (j0, alive()))

    sweep(bottom, i - 2)
    sweep(top, jnp.where(i >= 2, i - 3, -1))
    out_ref[...] = acc_ref[...].astype(out_ref.dtype)


def _stickbreak(u, *, batch, seq_len):
    nb = seq_len // BLOCK
    u3 = u.reshape(batch, seq_len, U_WIDTH)
    whole = functools.partial(pl.BlockSpec, (None, seq_len, SB_WIDTH), pipeline_mode=pl.Buffered(1))
    return pl.pallas_call(
        _sb_kernel,
        grid=(batch, nb),
        in_specs=[
            pl.BlockSpec((None, BLOCK, SB_WIDTH), lambda b, i: (b, i, COL_SBQ // SEG)),
            whole(lambda b, i: (b, 0, COL_SBK // SEG)),
            whole(lambda b, i: (b, 0, COL_SBV // SEG)),
        ],
        out_specs=pl.BlockSpec((None, BLOCK, SB_WIDTH), lambda b, i: (b, i, 0)),
        out_shape=jax.ShapeDtypeStruct((batch, seq_len, SB_WIDTH), BF16),
        scratch_shapes=[
            pltpu.VMEM((BLOCK, SB_WIDTH), F32),
            pltpu.VMEM((SB_HEADS, BLOCK, BLOCK), F32),
        ],
        compiler_params=pltpu.CompilerParams(
            dimension_semantics=("arbitrary", "arbitrary"),
            vmem_limit_bytes=V7X_VMEM_LIMIT),
        name="stickbreak",
    )(u3, u3, u3).reshape(batch * seq_len, SB_WIDTH)


def _out_proj_kernel(ml_ref, sb_ref, pool_ref, halo_ref, h_ref, wo_ref, pw_ref, ps_ref, g_ref,
                     out_ref, ext_ref, *, tm, seq_len):
    i = pl.program_id(0)
    base = lax.rem(i * tm, seq_len)
    pos = base + lax.broadcasted_iota(jnp.int32, (tm, 1), 0)
    validf = (pos >= META_PAD).astype(F32)

    ext_ref[0:POOL_HALO, :] = jnp.where(base == 0, 0.0, halo_ref[...].astype(F32))
    ext_ref[POOL_HALO:POOL_HALO + tm, :] = pool_ref[...].astype(F32) * validf

    pooled = []
    for g, w in enumerate(POOL_WINDOWS):
        lo, hi = g * POOL_CH, (g + 1) * POOL_CH
        uf = ext_ref[POOL_HALO:POOL_HALO + tm, lo:hi]
        s = uf
        for d in range(1, w):
            s = s + ext_ref[POOL_HALO - d:POOL_HALO - d + tm, lo:hi]
        cnt = jnp.clip(pos - (META_PAD - 1), 0, w).astype(F32)
        y = s / jnp.maximum(cnt, 1.0) - uf
        yg = jnp.dot(y.astype(BF16), pw_ref[g], preferred_element_type=F32) * ps_ref[g:g + 1, :]
        pooled.append(yg.astype(BF16))
    pool_out = jnp.concatenate(pooled, axis=1)

    y = (jnp.dot(ml_ref[...], wo_ref[0:ML_WIDTH, :], preferred_element_type=F32)
         + jnp.dot(sb_ref[...], wo_ref[ML_WIDTH:ML_WIDTH + SB_WIDTH, :], preferred_element_type=F32)
         + jnp.dot(pool_out, wo_ref[ML_WIDTH + SB_WIDTH:, :], preferred_element_type=F32))
    out_ref[...] = h_ref[...] + _rms_scale(y, g_ref[...])


def _out_proj(ml, sb, u, h, w_out, pool_w, pool_scale, g_post, layer, *, seq_len, tm):
    m = h.shape[0]
    halo_blocks = tm // POOL_HALO
    return pl.pallas_call(
        functools.partial(_out_proj_kernel, tm=tm, seq_len=seq_len),
        grid=(m // tm,),
        in_specs=[
            pl.BlockSpec((tm, ML_WIDTH), lambda i: (i, 0)),
            pl.BlockSpec((tm, SB_WIDTH), lambda i: (i, 0)),
            pl.BlockSpec((tm, POOL_WIDTH), lambda i: (i, COL_POOL // POOL_WIDTH)),
            pl.BlockSpec((POOL_HALO, POOL_WIDTH),
                         lambda i: (jnp.maximum(i * halo_blocks - 1, 0), COL_POOL // POOL_WIDTH)),
            pl.BlockSpec((tm, D_MODEL), lambda i: (i, 0)),
            pl.BlockSpec((None, D_MODEL, D_MODEL), lambda i: (layer, 0, 0), pipeline_mode=pl.Buffered(1)),
            pl.BlockSpec((None, len(POOL_WINDOWS), POOL_CH, POOL_CH), lambda i: (layer, 0, 0, 0)),
            pl.BlockSpec((len(POOL_WINDOWS), POOL_CH), lambda i: (0, 0)),
            pl.BlockSpec((1, D_MODEL), lambda i: (0, 0)),
        ],
        out_specs=pl.BlockSpec((tm, D_MODEL), lambda i: (i, 0)),
        out_shape=jax.ShapeDtypeStruct((m, D_MODEL), F32),
        scratch_shapes=[pltpu.VMEM((POOL_HALO + tm, POOL_WIDTH), F32)],
        compiler_params=pltpu.CompilerParams(
            dimension_semantics=("arbitrary",),
            vmem_limit_bytes=V7X_VMEM_LIMIT),
        name="out_proj",
    )(ml, sb, u, u, h, w_out, pool_w, pool_scale, g_post)


def _ffn_kernel(h_ref, gpre_ref, wg_ref, wu_ref, wd_ref, gpost_ref, out_ref, xn_ref, acc_ref):
    f = pl.program_id(1)

    @pl.when(f == 0)
    def _():
        xn_ref[...] = _rms_scale(h_ref[...], gpre_ref[...]).astype(BF16)
        acc_ref[...] = jnp.zeros_like(acc_ref)

    xn = xn_ref[...]
    gate = jnp.dot(xn, wg_ref[...], preferred_element_type=F32)
    up = jnp.dot(xn, wu_ref[...], preferred_element_type=F32)
    act = (gate * jax.nn.sigmoid(gate) * up).astype(BF16)
    acc_ref[...] += jnp.dot(act, wd_ref[...], preferred_element_type=F32)

    @pl.when(f == pl.num_programs(1) - 1)
    def _():
        out_ref[...] = h_ref[...] + _rms_scale(acc_ref[...], gpost_ref[...])


def _ffn(h, g_pre, w_gate_up, w_down, g_post, layer, *, tm, tf):
    m = h.shape[0]
    nf = FFN_HIDDEN // tf
    return pl.pallas_call(
        _ffn_kernel,
        grid=(m // tm, nf),
        in_specs=[
            pl.BlockSpec((tm, D_MODEL), lambda i, f: (i, 0)),
            pl.BlockSpec((1, D_MODEL), lambda i, f: (0, 0)),
            pl.BlockSpec((None, D_MODEL, tf), lambda i, f: (layer, 0, f)),
            pl.BlockSpec((None, D_MODEL, tf), lambda i, f: (layer, 0, nf + f)),
            pl.BlockSpec((None, tf, D_MODEL), lambda i, f: (layer, f, 0)),
            pl.BlockSpec((1, D_MODEL), lambda i, f: (0, 0)),
        ],
        out_specs=pl.BlockSpec((tm, D_MODEL), lambda i, f: (i, 0)),
        out_shape=jax.ShapeDtypeStruct((m, D_MODEL), F32),
        scratch_shapes=[pltpu.VMEM((tm, D_MODEL), BF16), pltpu.VMEM((tm, D_MODEL), F32)],
        compiler_params=pltpu.CompilerParams(
            dimension_semantics=("arbitrary", "arbitrary"),
            vmem_limit_bytes=V7X_VMEM_LIMIT),
        name="ffn",
    )(h, g_pre, w_gate_up, w_gate_up, w_down, g_post)


def _row_tile(seq_len, target):
    best = 16
    for t in range(16, target + 1, 16):
        if seq_len % t == 0:
            best = t
    return best


def kernel(x, meta_tokens, w_in, ml_conv_w, ml_igate_b, ml_fgate_b, pool_w, pool_scale, w_out,
           g_mix_pre, g_mix_post, g_ffn_pre, g_ffn_post, w_gate_up, w_down):
    batch, seq, d = x.shape
    depth = w_in.shape[0]
    seq_len = BLOCK + seq
    meta = jnp.broadcast_to(meta_tokens[None].astype(x.dtype), (batch, N_META, d))
    h = jnp.concatenate([jnp.zeros((batch, META_PAD, d), x.dtype), meta, x], axis=1)
    h = h.reshape(batch * seq_len, d)

    w_in_r = _w_in_prep(w_in)
    w_out_bf, pool_w_bf = w_out.astype(BF16), pool_w.astype(BF16)
    w_gate_up_bf, w_down_bf = w_gate_up.astype(BF16), w_down.astype(BF16)
    tm_mm = _row_tile(seq_len, 640)
    tm_out = _row_tile(seq_len, 640)
    for l in range(depth):
        gate_bias = jnp.concatenate(
            [ml_igate_b[l], ml_fgate_b[l], jnp.zeros((GATE_W - N_GATES,), F32)])[None, :]
        u, gates = _in_proj(h, g_mix_pre[l][None, :], w_in_r, l, seq_len=seq_len, tm=tm_mm)
        ml = _mlstm(u, gates, ml_conv_w[l], gate_bias, batch=batch, seq_len=seq_len)
        sb = _stickbreak(u, batch=batch, seq_len=seq_len)
        h = _out_proj(ml, sb, u, h, w_out_bf, pool_w_bf, pool_scale[l], g_mix_post[l][None, :], l,
                      seq_len=seq_len, tm=tm_out)
        h = _ffn(h, g_ffn_pre[l][None, :], w_gate_up_bf, w_down_bf, g_ffn_post[l][None, :], l,
                 tm=tm_mm, tf=512)
    return h.reshape(batch, seq_len, d)[:, BLOCK:, :]
```

```python
import functools
import math

import jax
import jax.numpy as jnp
from jax import lax
from jax.experimental import pallas as pl
from jax.experimental.pallas import tpu as pltpu

F32 = jnp.float32
BF16 = jnp.bfloat16

D_MODEL = 2048
N_META = 16
BLOCK = 128
META_PAD = BLOCK - N_META
EPS = 1e-6

HEAD_DIM = 128
ML_HEADS = 6
ML_DQK = 64
ML_QK_WIDTH = ML_HEADS * ML_DQK
ML_WIDTH = ML_HEADS * HEAD_DIM
ML_CONV = 4
SB_HEADS = 6
SB_WIDTH = SB_HEADS * HEAD_DIM
POOL_WINDOWS = (2, 4, 8, 16)
POOL_CH = 128
POOL_WIDTH = POOL_CH * len(POOL_WINDOWS)
POOL_HALO = 16
FFN_HIDDEN = 5632
IN_SIZES = (ML_QK_WIDTH, ML_QK_WIDTH, ML_WIDTH, ML_WIDTH, ML_HEADS, ML_HEADS,
            SB_WIDTH, SB_WIDTH, SB_WIDTH, POOL_WIDTH)
IN_WIDTH = sum(IN_SIZES)
IN_GATE_OFF = 2 * ML_QK_WIDTH + 2 * ML_WIDTH
N_GATES = 2 * ML_HEADS

SEG = 768
COL_QK, COL_MLV, COL_MLO, COL_SBQ, COL_SBK, COL_SBV = (i * SEG for i in range(6))
COL_POOL = 6 * SEG
U_WIDTH = COL_POOL + POOL_WIDTH
GATE_W = 128
N_IN = U_WIDTH + GATE_W
IN_CHUNKS = tuple((i * SEG, SEG) for i in range(6)) + ((COL_POOL, POOL_WIDTH),)

V7X_VMEM_LIMIT = 56 * 1024 * 1024

SB_DEAD_LOG = -104.0
SB_TOP_ROWS = 32


def _rms_scale(x, g):
    ms = jnp.mean(x * x, axis=-1, keepdims=True)
    return x * lax.rsqrt(ms + EPS) * g


def _w_in_prep_kernel(w_ref, out_ref):
    x = w_ref[...]
    out_ref[:, 0:IN_GATE_OFF] = x[:, 0:IN_GATE_OFF].astype(BF16)
    out_ref[:, IN_GATE_OFF:U_WIDTH] = x[:, IN_GATE_OFF + N_GATES:IN_WIDTH].astype(BF16)
    lane = lax.broadcasted_iota(jnp.int32, (1, GATE_W), 1)
    gates = jnp.where(lane < N_GATES, x[:, IN_GATE_OFF:IN_GATE_OFF + GATE_W], 0.0)
    out_ref[:, U_WIDTH:N_IN] = gates.astype(BF16)


def _w_in_prep(w_in, *, tr=256):
    depth = w_in.shape[0]
    return pl.pallas_call(
        _w_in_prep_kernel,
        grid=(depth, D_MODEL // tr),
        in_specs=[pl.BlockSpec((None, tr, IN_WIDTH), lambda l, r: (l, r, 0))],
        out_specs=pl.BlockSpec((None, tr, N_IN), lambda l, r: (l, r, 0)),
        out_shape=jax.ShapeDtypeStruct((depth, D_MODEL, N_IN), BF16),
        compiler_params=pltpu.CompilerParams(
            dimension_semantics=("arbitrary", "arbitrary"),
            vmem_limit_bytes=V7X_VMEM_LIMIT),
        name="w_in_prep",
    )(w_in)


def _in_proj_kernel(h_ref, g_ref, w_ref, u_ref, gate_ref, xn_ref, *, tm, seq_len):
    i = pl.program_id(0)
    xn_ref[...] = _rms_scale(h_ref[...], g_ref[...]).astype(BF16)
    pos = lax.rem(i * tm, seq_len) + lax.broadcasted_iota(jnp.int32, (tm, 1), 0)
    validf = (pos >= META_PAD).astype(F32)
    for c0, cw in IN_CHUNKS:
        acc = jnp.dot(xn_ref[...], w_ref[:, c0:c0 + cw], preferred_element_type=F32)
        u_ref[:, c0:c0 + cw] = (acc * validf).astype(BF16)
    acc = jnp.dot(xn_ref[...], w_ref[:, U_WIDTH:N_IN], preferred_element_type=F32)
    gate_ref[...] = acc * validf


def _in_proj(h, g, w_all, layer, *, seq_len, tm):
    m = h.shape[0]
    return pl.pallas_call(
        functools.partial(_in_proj_kernel, tm=tm, seq_len=seq_len),
        grid=(m // tm,),
        in_specs=[
            pl.BlockSpec((tm, D_MODEL), lambda i: (i, 0)),
            pl.BlockSpec((1, D_MODEL), lambda i: (0, 0)),
            pl.BlockSpec((None, D_MODEL, N_IN), lambda i: (layer, 0, 0), pipeline_mode=pl.Buffered(1)),
        ],
        out_specs=[
            pl.BlockSpec((tm, U_WIDTH), lambda i: (i, 0)),
            pl.BlockSpec((tm, GATE_W), lambda i: (i, 0)),
        ],
        out_shape=[jax.ShapeDtypeStruct((m, U_WIDTH), BF16), jax.ShapeDtypeStruct((m, GATE_W), F32)],
        scratch_shapes=[pltpu.VMEM((tm, D_MODEL), BF16)],
        compiler_params=pltpu.CompilerParams(
            dimension_semantics=("arbitrary",),
            vmem_limit_bytes=V7X_VMEM_LIMIT),
        name="in_proj",
    )(h, g, w_all)


def _log_sigmoid(x):
    return jnp.minimum(x, 0.0) - jnp.log(1.0 + jnp.exp(-jnp.abs(x)))


def _split3_bf16(x):
    hi = x.astype(BF16)
    r1 = x - hi.astype(F32)
    mid = r1.astype(BF16)
    lo = (r1 - mid.astype(F32)).astype(BF16)
    return hi, mid, lo


def _mlstm_stages(c, qkprev_ref, qk_ref, v_ref, o_ref, gate_ref, convw_ref, gbias_ref, out_ref,
                  ctn_ref, m_ref):
    T = BLOCK
    NH = ML_HEADS

    cur = qk_ref[...]
    prev = jnp.where(c == 0, jnp.zeros_like(cur), qkprev_ref[...])
    both = jnp.concatenate([prev, cur], axis=0)
    rr = lax.broadcasted_iota(jnp.int32, ((ML_CONV - 1) * T, 2 * T), 0)
    cc = lax.broadcasted_iota(jnp.int32, ((ML_CONV - 1) * T, 2 * T), 1)
    delay = (ML_CONV - 1) - rr // T
    shift = (cc == T + rr % T - delay).astype(BF16)
    delayed = jnp.dot(shift, both, preferred_element_type=F32)
    yield
    conv = convw_ref[ML_CONV - 1:ML_CONV, :] * cur.astype(F32)
    for j in range(ML_CONV - 1):
        conv = conv + convw_ref[j:j + 1, :] * delayed[j * T:(j + 1) * T, :]
    qk = conv * jax.nn.sigmoid(conv)
    q_all = qk[:, :ML_QK_WIDTH]
    k_all = qk[:, ML_QK_WIDTH:] * (1.0 / math.sqrt(ML_DQK))

    row = lax.broadcasted_iota(jnp.int32, (T, 1), 0)
    valid = (c * T + row) >= META_PAD
    gpre = gate_ref[...] + gbias_ref[...]
    log_i = jnp.where(valid, gpre, -jnp.inf)
    log_f = jnp.where(valid, _log_sigmoid(gpre), 0.0)
    r_i = lax.broadcasted_iota(jnp.int32, (T, T), 0)
    c_i = lax.broadcasted_iota(jnp.int32, (T, T), 1)
    causal = c_i <= r_i
    tri = causal.astype(F32)
    bcum = jnp.dot(tri, log_f, preferred_element_type=F32, precision=lax.Precision.HIGHEST)
    bcum_t = bcum.T
    log_i_t = log_i.T
    e_r = lax.broadcasted_iota(jnp.int32, (T, 2 * NH * T), 0)
    e_c = lax.broadcasted_iota(jnp.int32, (T, 2 * NH * T), 1)
    spread = (e_r == e_c // T).astype(BF16)
    lane_g = lax.broadcasted_iota(jnp.int32, (1, T), 1)
    packed = jnp.where(lane_g < NH, gpre, bcum)
    on_lanes = sum(jnp.dot(part, spread, preferred_element_type=F32) for part in _split3_bf16(packed))
    yield

    lane = lax.broadcasted_iota(jnp.int32, (1, T), 1)
    heads = range(ML_HEADS)
    pairs = range(ML_HEADS // 2)
    halves = [(lane >= ML_DQK * e) & (lane < ML_DQK * (e + 1)) for e in range(2)]
    k_pairs = [k_all[:, 128 * p:128 * (p + 1)] for p in pairs]
    k_bf = [k.astype(BF16) for k in k_pairs]
    kt_bf = [k.T.astype(BF16) for k in k_pairs]
    ctn_old = [ctn_ref[p] for p in pairs]
    ctn_bf = [s.astype(BF16) for s in ctn_old]
    qms = [jnp.where(halves[h % 2], q_all[:, 128 * (h // 2):128 * (h // 2 + 1)], 0.0).astype(BF16)
           for h in heads]
    vs = [v_ref[:, 128 * h:128 * (h + 1)] for h in heads]
    ones_bf = jnp.ones((T, T), BF16)

    qk = [lax.dot_general(qms[h], k_bf[h // 2], (((1,), (1,)), ((), ())), preferred_element_type=F32)
          for h in heads]
    q_state = [jnp.dot(qms[h], ctn_bf[h // 2], preferred_element_type=F32) for h in heads]
    yield

    s_mats, a_inters, floor_dens, wvs, a_olds, a_locs, m_news = [], [], [], [], [], [], []
    for h in heads:
        b_t = on_lanes[:, (NH + h) * T:(NH + h + 1) * T]
        ig_t = jnp.where(valid, on_lanes[:, h * T:(h + 1) * T], -jnp.inf)
        a_row = log_i_t[h:h + 1, :] - bcum_t[NH + h:NH + h + 1, :]
        m_prev = m_ref[h:h + 1, :]
        b_last = b_t[T - 1:T, :]
        dm = jnp.where(causal, b_t + a_row, -jnp.inf)
        inter = b_t + m_prev
        m_t = jnp.maximum(inter, jnp.max(dm, axis=1, keepdims=True))
        s_mats.append(qk[h] * jnp.exp(dm - m_t))
        a_inters.append(jnp.exp(inter - m_t))
        floor_dens.append(jnp.exp(-m_t))
        g_t = b_last + ig_t - b_t
        m_loc = jnp.max(g_t, axis=0, keepdims=True)
        w_t = jnp.exp(g_t - m_loc)
        m_new = jnp.maximum(b_last + m_prev, m_loc)
        a_olds.append(jnp.exp(b_last + m_prev - m_new))
        a_locs.append(jnp.exp(m_loc - m_new))
        m_news.append(m_new)
        wvs.append(jnp.concatenate([(w_t * vs[h].astype(F32)).astype(BF16), w_t.astype(BF16)], axis=1))
    yield

    intra = [jnp.dot(s_mats[h].astype(BF16), jnp.concatenate([vs[h], ones_bf], axis=1),
                     preferred_element_type=F32) for h in heads]
    ctn_loc = [jnp.dot(kt_bf[h // 2], wvs[h], preferred_element_type=F32) for h in heads]
    yield

    outs = []
    for h in heads:
        num = intra[h][:, :T] + a_inters[h] * q_state[h][:, :T]
        den = intra[h][:, T:] + a_inters[h] * q_state[h][:, T:]
        hm = num / jnp.maximum(jnp.abs(den), floor_dens[h])
        outs.append(hm * jax.nn.sigmoid(o_ref[:, 128 * h:128 * (h + 1)].astype(F32)))
    out_ref[...] = jnp.concatenate(outs, axis=1).astype(out_ref.dtype)

    for h in heads:
        p, e = divmod(h, 2)
        lo, hi = ML_DQK * e, ML_DQK * (e + 1)
        for half in (slice(0, T), slice(T, 2 * T)):
            ctn_ref[p, lo:hi, half] = (a_olds[h] * ctn_old[p][lo:hi, half]
                                       + a_locs[h] * ctn_loc[h][lo:hi, half])
        m_ref[h:h + 1, :] = m_news[h]


def _sb_tile_stages(q_ref, k_ref, v_ref, acc_ref, rest_ref, later_ones, tiles, *, assign):
    T = BLOCK
    scale = 1.0 / math.sqrt(HEAD_DIM)
    heads = range(SB_HEADS)
    cols = [slice(HEAD_DIM * h, HEAD_DIM * (h + 1)) for h in heads]
    kofs = lax.broadcasted_iota(jnp.int32, (1, T), 1)

    zs = {}
    for t, (start, _, (r0, r1)) in enumerate(tiles):
        for h in heads:
            zs[t, h] = lax.dot_general(q_ref[r0:r1, cols[h]], k_ref[pl.ds(start, T), cols[h]],
                                       (((1,), (1,)), ((), ())), preferred_element_type=F32) * scale
    yield
    his, los, log_betas = {}, {}, {}
    for t, (start, is_diag, (r0, r1)) in enumerate(tiles):
        live = (start + kofs) >= META_PAD
        if is_diag:
            live = live & (kofs < r0 + lax.broadcasted_iota(jnp.int32, (r1 - r0, T), 0))
        for h in heads:
            z = zs[t, h]
            sp = jnp.maximum(z, 0.0) + jnp.log(1.0 + jnp.exp(-jnp.abs(z)))
            l1 = jnp.where(live, -sp, 0.0)
            his[t, h] = l1.astype(BF16)
            los[t, h] = (l1 - his[t, h].astype(F32)).astype(BF16)
            log_betas[t, h] = jnp.where(live, z - sp, -jnp.inf)
    yield
    sums = {key: jnp.dot(his[key], later_ones, preferred_element_type=F32)
            + jnp.dot(los[key], later_ones, preferred_element_type=F32) for key in his}
    yield
    probs = {}
    for h in heads:
        rows0 = tiles[0][2]
        rest = None if assign else rest_ref[h, rows0[0]:rows0[1], :]
        for t, (_, _, (r0, r1)) in enumerate(tiles):
            lo, hi = r0 - rows0[0], r1 - rows0[0]
            log_a = log_betas[t, h] + sums[t, h][:, :T]
            new = sums[t, h][:, T:]
            if rest is not None:
                log_a = log_a + rest[lo:hi]
                new = new + rest[lo:hi]
                pieces = ([rest[:lo]] if lo else []) + [new] + ([rest[hi:]] if hi < rest.shape[0] else [])
                new = pieces[0] if len(pieces) == 1 else jnp.concatenate(pieces, axis=0)
            probs[t, h] = jnp.exp(log_a).astype(BF16)
            rest = new
        rest_ref[h, rows0[0]:rows0[1], :] = rest
    yield
    for t, (start, _, (r0, r1)) in enumerate(tiles):
        for h in heads:
            pv = jnp.dot(probs[t, h], v_ref[pl.ds(start, T), cols[h]], preferred_element_type=F32)
            if assign and t == 0:
                acc_ref[r0:r1, cols[h]] = pv
            else:
                acc_ref[r0:r1, cols[h]] += pv


def _run_interleaved(*streams):
    live = list(streams)
    while live:
        for stream in list(live):
            try:
                next(stream)
            except StopIteration:
                live.remove(stream)


def _mixers_kernel(qkprev_ref, qk_ref, mlv_ref, mlo_ref, gate_ref, convw_ref, gbias_ref,
                   sbq_ref, sbk_ref, sbv_ref, ml_out_ref, sb_out_ref,
                   ctn_ref, m_ref, acc_ref, rest_ref):
    i = pl.program_id(1)
    T = BLOCK

    @pl.when(i == 0)
    def _():
        ctn_ref[...] = jnp.zeros_like(ctn_ref)
        m_ref[...] = jnp.zeros_like(m_ref)

    r_w = lax.broadcasted_iota(jnp.int32, (T, 2 * T), 0)
    c_w = lax.broadcasted_iota(jnp.int32, (T, 2 * T), 1)
    later_ones = ((r_w > c_w) | (c_w >= T)).astype(BF16)
    sb_tiles = functools.partial(_sb_tile_stages, sbq_ref, sbk_ref, sbv_ref, acc_ref, rest_ref, later_ones)
    mlstm = functools.partial(_mlstm_stages, i, qkprev_ref, qk_ref, mlv_ref, mlo_ref, gate_ref, convw_ref,
                              gbias_ref, ml_out_ref, ctn_ref, m_ref)
    full, top, bottom = (0, T), (0, SB_TOP_ROWS), (SB_TOP_ROWS, T)
    start = lambda back: pl.multiple_of((i - back) * T, T)

    @pl.when(i == 0)
    def _():
        _run_interleaved(mlstm(), sb_tiles([(start(0), True, full)], assign=True))

    @pl.when(i == 1)
    def _():
        _run_interleaved(mlstm(), sb_tiles([(start(0), True, full), (start(1), False, full)], assign=True))

    @pl.when(i >= 2)
    def _():
        _run_interleaved(mlstm(), sb_tiles([(start(0), True, full), (start(1), False, full),
                                            (start(2), False, top)], assign=True))

    def sweep(rows, j0):
        def alive():
            worst = rest_ref[0, rows[0]:rows[1], :]
            for h in range(1, SB_HEADS):
                worst = jnp.maximum(worst, rest_ref[h, rows[0]:rows[1], :])
            return jnp.max(worst)

        def cond(carry):
            j, worst = carry
            return jnp.logical_and(j >= 0, worst > SB_DEAD_LOG)

        def body(carry):
            j, _ = carry
            _run_interleaved(sb_tiles([(pl.multiple_of(j * T, T), False, rows)], assign=False))
            return j - 1, alive()

        lax.while_loop(cond, body, (j0, alive()))

    sweep(bottom, i - 2)
    sweep(top, jnp.where(i >= 2, i - 3, -1))
    sb_out_ref[...] = acc_ref[...].astype(sb_out_ref.dtype)


def _mixers(u, gates, conv_w, gate_bias, *, batch, seq_len):
    nb = seq_len // BLOCK
    u3 = u.reshape(batch, seq_len, U_WIDTH)
    g3 = gates.reshape(batch, seq_len, GATE_W)
    blk = lambda width, col: pl.BlockSpec((None, BLOCK, width), lambda b, i: (b, i, col))
    whole = lambda col: pl.BlockSpec((None, seq_len, SB_WIDTH), lambda b, i: (b, 0, col),
                                     pipeline_mode=pl.Buffered(1))
    ml, sb = pl.pallas_call(
        _mixers_kernel,
        grid=(batch, nb),
        in_specs=[
            pl.BlockSpec((None, BLOCK, SEG), lambda b, i: (b, jnp.maximum(i - 1, 0), COL_QK // SEG)),
            blk(SEG, COL_QK // SEG),
            blk(SEG, COL_MLV // SEG),
            blk(SEG, COL_MLO // SEG),
            blk(GATE_W, 0),
            pl.BlockSpec((ML_CONV, 2 * ML_QK_WIDTH), lambda b, i: (0, 0)),
            pl.BlockSpec((1, GATE_W), lambda b, i: (0, 0)),
            blk(SB_WIDTH, COL_SBQ // SEG),
            whole(COL_SBK // SEG),
            whole(COL_SBV // SEG),
        ],
        out_specs=[blk(ML_WIDTH, 0), blk(SB_WIDTH, 0)],
        out_shape=[jax.ShapeDtypeStruct((batch, seq_len, ML_WIDTH), BF16),
                   jax.ShapeDtypeStruct((batch, seq_len, SB_WIDTH), BF16)],
        scratch_shapes=[
            pltpu.VMEM((ML_HEADS // 2, 128, 256), F32),
            pltpu.VMEM((8, 128), F32),
            pltpu.VMEM((BLOCK, SB_WIDTH), F32),
            pltpu.VMEM((SB_HEADS, BLOCK, BLOCK), F32),
        ],
        compiler_params=pltpu.CompilerParams(
            dimension_semantics=("arbitrary", "arbitrary"),
            vmem_limit_bytes=V7X_VMEM_LIMIT),
        name="mixers",
    )(u3, u3, u3, u3, g3, conv_w, gate_bias, u3, u3, u3)
    return ml.reshape(batch * seq_len, ML_WIDTH), sb.reshape(batch * seq_len, SB_WIDTH)


def _out_proj_kernel(ml_ref, sb_ref, pool_ref, halo_ref, h_ref, wo_ref, pw_ref, ps_ref, g_ref,
                     out_ref, ext_ref, *, tm, seq_len):
    i = pl.program_id(0)
    base = lax.rem(i * tm, seq_len)
    pos = base + lax.broadcasted_iota(jnp.int32, (tm, 1), 0)
    validf = (pos >= META_PAD).astype(F32)

    ext_ref[0:POOL_HALO, :] = jnp.where(base == 0, 0.0, halo_ref[...].astype(F32))
    ext_ref[POOL_HALO:POOL_HALO + tm, :] = pool_ref[...].astype(F32) * validf

    n_chunks = len(POOL_WINDOWS)
    cw = D_MODEL // n_chunks
    partial = []
    pooled = []
    for g, w in enumerate(POOL_WINDOWS):
        cols = slice(g * cw, (g + 1) * cw)
        partial.append(
            jnp.dot(ml_ref[...], wo_ref[0:ML_WIDTH, cols], preferred_element_type=F32)
            + jnp.dot(sb_ref[...], wo_ref[ML_WIDTH:ML_WIDTH + SB_WIDTH, cols], preferred_element_type=F32))
        lo, hi = g * POOL_CH, (g + 1) * POOL_CH
        uf = ext_ref[POOL_HALO:POOL_HALO + tm, lo:hi]
        s = uf
        for d in range(1, w):
            s = s + ext_ref[POOL_HALO - d:POOL_HALO - d + tm, lo:hi]
        cnt = jnp.clip(pos - (META_PAD - 1), 0, w).astype(F32)
        centred = s / jnp.maximum(cnt, 1.0) - uf
        yg = jnp.dot(centred.astype(BF16), pw_ref[g], preferred_element_type=F32) * ps_ref[g:g + 1, :]
        pooled.append(yg.astype(BF16))
    pool_out = jnp.concatenate(pooled, axis=1)

    y = jnp.concatenate(
        [partial[c] + jnp.dot(pool_out, wo_ref[ML_WIDTH + SB_WIDTH:, c * cw:(c + 1) * cw],
                              preferred_element_type=F32) for c in range(n_chunks)], axis=1)
    out_ref[...] = h_ref[...] + _rms_scale(y, g_ref[...])


def _out_proj(ml, sb, u, h, w_out, pool_w, pool_scale, g_post, layer, *, seq_len, tm):
    m = h.shape[0]
    halo_blocks = tm // POOL_HALO
    return pl.pallas_call(
        functools.partial(_out_proj_kernel, tm=tm, seq_len=seq_len),
        grid=(m // tm,),
        in_specs=[
            pl.BlockSpec((tm, ML_WIDTH), lambda i: (i, 0)),
            pl.BlockSpec((tm, SB_WIDTH), lambda i: (i, 0)),
            pl.BlockSpec((tm, POOL_WIDTH), lambda i: (i, COL_POOL // POOL_WIDTH)),
            pl.BlockSpec((POOL_HALO, POOL_WIDTH),
                         lambda i: (jnp.maximum(i * halo_blocks - 1, 0), COL_POOL // POOL_WIDTH)),
            pl.BlockSpec((tm, D_MODEL), lambda i: (i, 0)),
            pl.BlockSpec((None, D_MODEL, D_MODEL), lambda i: (layer, 0, 0), pipeline_mode=pl.Buffered(1)),
            pl.BlockSpec((None, len(POOL_WINDOWS), POOL_CH, POOL_CH), lambda i: (layer, 0, 0, 0)),
            pl.BlockSpec((len(POOL_WINDOWS), POOL_CH), lambda i: (0, 0)),
            pl.BlockSpec((1, D_MODEL), lambda i: (0, 0)),
        ],
        out_specs=pl.BlockSpec((tm, D_MODEL), lambda i: (i, 0)),
        out_shape=jax.ShapeDtypeStruct((m, D_MODEL), F32),
        scratch_shapes=[pltpu.VMEM((POOL_HALO + tm, POOL_WIDTH), F32)],
        compiler_params=pltpu.CompilerParams(
            dimension_semantics=("arbitrary",),
            vmem_limit_bytes=V7X_VMEM_LIMIT),
        name="out_proj",
    )(ml, sb, u, u, h, w_out, pool_w, pool_scale, g_post)


def _ffn_kernel(h_ref, gpre_ref, wg_ref, wu_ref, wd_ref, gpost_ref, out_ref, xn_ref, acc_ref):
    f = pl.program_id(1)

    @pl.when(f == 0)
    def _():
        xn_ref[...] = _rms_scale(h_ref[...], gpre_ref[...]).astype(BF16)
        acc_ref[...] = jnp.zeros_like(acc_ref)

    xn = xn_ref[...]
    gate = jnp.dot(xn, wg_ref[...], preferred_element_type=F32)
    up = jnp.dot(xn, wu_ref[...], preferred_element_type=F32)
    act = (gate * jax.nn.sigmoid(gate) * up).astype(BF16)
    acc_ref[...] += jnp.dot(act, wd_ref[...], preferred_element_type=F32)

    @pl.when(f == pl.num_programs(1) - 1)
    def _():
        out_ref[...] = h_ref[...] + _rms_scale(acc_ref[...], gpost_ref[...])


def _ffn(h, g_pre, w_gate_up, w_down, g_post, layer, *, tm, tf):
    m = h.shape[0]
    nf = FFN_HIDDEN // tf
    return pl.pallas_call(
        _ffn_kernel,
        grid=(m // tm, nf),
        in_specs=[
            pl.BlockSpec((tm, D_MODEL), lambda i, f: (i, 0)),
            pl.BlockSpec((1, D_MODEL), lambda i, f: (0, 0)),
            pl.BlockSpec((None, D_MODEL, tf), lambda i, f: (layer, 0, f)),
            pl.BlockSpec((None, D_MODEL, tf), lambda i, f: (layer, 0, nf + f)),
            pl.BlockSpec((None, tf, D_MODEL), lambda i, f: (layer, f, 0)),
            pl.BlockSpec((1, D_MODEL), lambda i, f: (0, 0)),
        ],
        out_specs=pl.BlockSpec((tm, D_MODEL), lambda i, f: (i, 0)),
        out_shape=jax.ShapeDtypeStruct((m, D_MODEL), F32),
        scratch_shapes=[pltpu.VMEM((tm, D_MODEL), BF16), pltpu.VMEM((tm, D_MODEL), F32)],
        compiler_params=pltpu.CompilerParams(
            dimension_semantics=("arbitrary", "arbitrary"),
            vmem_limit_bytes=V7X_VMEM_LIMIT),
        name="ffn",
    )(h, g_pre, w_gate_up, w_gate_up, w_down, g_post)


def _row_tile(seq_len, target):
    best = 16
    for t in range(16, target + 1, 16):
        if seq_len % t == 0:
            best = t
    return best


def kernel(x, meta_tokens, w_in, ml_conv_w, ml_igate_b, ml_fgate_b, pool_w, pool_scale, w_out,
           g_mix_pre, g_mix_post, g_ffn_pre, g_ffn_post, w_gate_up, w_down):
    batch, seq, d = x.shape
    depth = w_in.shape[0]
    seq_len = BLOCK + seq
    meta = jnp.broadcast_to(meta_tokens[None].astype(x.dtype), (batch, N_META, d))
    h = jnp.concatenate([jnp.zeros((batch, META_PAD, d), x.dtype), meta, x], axis=1)
    h = h.reshape(batch * seq_len, d)

    w_in_r = _w_in_prep(w_in)
    w_out_bf, pool_w_bf = w_out.astype(BF16), pool_w.astype(BF16)
    w_gate_up_bf, w_down_bf = w_gate_up.astype(BF16), w_down.astype(BF16)
    tm_mm = _row_tile(seq_len, 640)
    tm_out = _row_tile(seq_len, 640)
    for l in range(depth):
        gate_bias = jnp.concatenate(
            [ml_igate_b[l], ml_fgate_b[l], jnp.zeros((GATE_W - N_GATES,), F32)])[None, :]
        u, gates = _in_proj(h, g_mix_pre[l][None, :], w_in_r, l, seq_len=seq_len, tm=tm_mm)
        ml, sb = _mixers(u, gates, ml_conv_w[l], gate_bias, batch=batch, seq_len=seq_len)
        h = _out_proj(ml, sb, u, h, w_out_bf, pool_w_bf, pool_scale[l], g_mix_post[l][None, :], l,
                      seq_len=seq_len, tm=tm_out)
        h = _ffn(h, g_ffn_pre[l][None, :], w_gate_up_bf, w_down_bf, g_ffn_post[l][None, :], l,
                 tm=tm_mm, tf=512)
    return h.reshape(batch, seq_len, d)[:, BLOCK:, :]
```

```python
import functools
import math

import jax
import jax.numpy as jnp
from jax import lax
from jax.experimental import pallas as pl
from jax.experimental.pallas import tpu as pltpu

F32 = jnp.float32
BF16 = jnp.bfloat16

D_MODEL = 2048
N_META = 16
BLOCK = 128
META_PAD = BLOCK - N_META
EPS = 1e-6

HEAD_DIM = 128
ML_HEADS = 6
ML_DQK = 64
ML_QK_WIDTH = ML_HEADS * ML_DQK
ML_WIDTH = ML_HEADS * HEAD_DIM
ML_CONV = 4
SB_HEADS = 6
SB_WIDTH = SB_HEADS * HEAD_DIM
POOL_WINDOWS = (2, 4, 8, 16)
POOL_CH = 128
POOL_WIDTH = POOL_CH * len(POOL_WINDOWS)
POOL_HALO = 16
FFN_HIDDEN = 5632
IN_SIZES = (ML_QK_WIDTH, ML_QK_WIDTH, ML_WIDTH, ML_WIDTH, ML_HEADS, ML_HEADS,
            SB_WIDTH, SB_WIDTH, SB_WIDTH, POOL_WIDTH)
IN_WIDTH = sum(IN_SIZES)
IN_GATE_OFF = 2 * ML_QK_WIDTH + 2 * ML_WIDTH
N_GATES = 2 * ML_HEADS

SEG = 768
COL_QK, COL_MLV, COL_MLO, COL_SBQ, COL_SBK, COL_SBV = (i * SEG for i in range(6))
COL_POOL = 6 * SEG
U_WIDTH = COL_POOL + POOL_WIDTH
GATE_W = 128
N_IN = U_WIDTH + GATE_W
IN_CHUNKS = tuple((i * SEG, SEG) for i in range(6)) + ((COL_POOL, POOL_WIDTH),)

V7X_VMEM_LIMIT = 56 * 1024 * 1024

SB_DEAD_LOG = -104.0
SB_TOP_ROWS = 32


def _rms_scale(x, g):
    ms = jnp.mean(x * x, axis=-1, keepdims=True)
    return x * lax.rsqrt(ms + EPS) * g


def _w_in_prep_kernel(w_ref, out_ref):
    x = w_ref[...]
    out_ref[:, 0:IN_GATE_OFF] = x[:, 0:IN_GATE_OFF].astype(BF16)
    out_ref[:, IN_GATE_OFF:U_WIDTH] = x[:, IN_GATE_OFF + N_GATES:IN_WIDTH].astype(BF16)
    lane = lax.broadcasted_iota(jnp.int32, (1, GATE_W), 1)
    gates = jnp.where(lane < N_GATES, x[:, IN_GATE_OFF:IN_GATE_OFF + GATE_W], 0.0)
    out_ref[:, U_WIDTH:N_IN] = gates.astype(BF16)


def _w_in_prep(w_in, *, tr=256):
    depth = w_in.shape[0]
    return pl.pallas_call(
        _w_in_prep_kernel,
        grid=(depth, D_MODEL // tr),
        in_specs=[pl.BlockSpec((None, tr, IN_WIDTH), lambda l, r: (l, r, 0))],
        out_specs=pl.BlockSpec((None, tr, N_IN), lambda l, r: (l, r, 0)),
        out_shape=jax.ShapeDtypeStruct((depth, D_MODEL, N_IN), BF16),
        compiler_params=pltpu.CompilerParams(
            dimension_semantics=("arbitrary", "arbitrary"),
            vmem_limit_bytes=V7X_VMEM_LIMIT),
        name="w_in_prep",
    )(w_in)


def _in_proj_kernel(h_ref, g_ref, w_ref, u_ref, gate_ref, xn_ref, *, tm, seq_len):
    i = pl.program_id(0)
    xn_ref[...] = _rms_scale(h_ref[...], g_ref[...]).astype(BF16)
    pos = lax.rem(i * tm, seq_len) + lax.broadcasted_iota(jnp.int32, (tm, 1), 0)
    validf = (pos >= META_PAD).astype(F32)
    for c0, cw in IN_CHUNKS:
        acc = jnp.dot(xn_ref[...], w_ref[:, c0:c0 + cw], preferred_element_type=F32)
        u_ref[:, c0:c0 + cw] = (acc * validf).astype(BF16)
    acc = jnp.dot(xn_ref[...], w_ref[:, U_WIDTH:N_IN], preferred_element_type=F32)
    gate_ref[...] = acc * validf


def _in_proj(h, g, w_all, layer, *, seq_len, tm):
    m = h.shape[0]
    return pl.pallas_call(
        functools.partial(_in_proj_kernel, tm=tm, seq_len=seq_len),
        grid=(m // tm,),
        in_specs=[
            pl.BlockSpec((tm, D_MODEL), lambda i: (i, 0)),
            pl.BlockSpec((1, D_MODEL), lambda i: (0, 0)),
            pl.BlockSpec((None, D_MODEL, N_IN), lambda i: (layer, 0, 0), pipeline_mode=pl.Buffered(1)),
        ],
        out_specs=[
            pl.BlockSpec((tm, U_WIDTH), lambda i: (i, 0)),
            pl.BlockSpec((tm, GATE_W), lambda i: (i, 0)),
        ],
        out_shape=[jax.ShapeDtypeStruct((m, U_WIDTH), BF16), jax.ShapeDtypeStruct((m, GATE_W), F32)],
        scratch_shapes=[pltpu.VMEM((tm, D_MODEL), BF16)],
        compiler_params=pltpu.CompilerParams(
            dimension_semantics=("arbitrary",),
            vmem_limit_bytes=V7X_VMEM_LIMIT),
        name="in_proj",
    )(h, g, w_all)


def _log_sigmoid(x):
    return jnp.minimum(x, 0.0) - jnp.log(1.0 + jnp.exp(-jnp.abs(x)))


def _split3_bf16(x):
    hi = x.astype(BF16)
    r1 = x - hi.astype(F32)
    mid = r1.astype(BF16)
    lo = (r1 - mid.astype(F32)).astype(BF16)
    return hi, mid, lo


def _mlstm_stages(c, qkprev_ref, qk_ref, v_ref, o_ref, gate_ref, convw_ref, gbias_ref, out_ref,
                  ctn_ref, m_ref):
    T = BLOCK
    NH = ML_HEADS

    cur = qk_ref[...]
    prev = jnp.where(c == 0, jnp.zeros_like(cur), qkprev_ref[...])
    both = jnp.concatenate([prev, cur], axis=0)
    rr = lax.broadcasted_iota(jnp.int32, ((ML_CONV - 1) * T, 2 * T), 0)
    cc = lax.broadcasted_iota(jnp.int32, ((ML_CONV - 1) * T, 2 * T), 1)
    delay = (ML_CONV - 1) - rr // T
    shift = (cc == T + rr % T - delay).astype(BF16)
    delayed = jnp.dot(shift, both, preferred_element_type=F32)
    yield
    conv = convw_ref[ML_CONV - 1:ML_CONV, :] * cur.astype(F32)
    for j in range(ML_CONV - 1):
        conv = conv + convw_ref[j:j + 1, :] * delayed[j * T:(j + 1) * T, :]
    qk = conv * jax.nn.sigmoid(conv)
    q_all = qk[:, :ML_QK_WIDTH]
    k_all = qk[:, ML_QK_WIDTH:] * (1.0 / math.sqrt(ML_DQK))

    row = lax.broadcasted_iota(jnp.int32, (T, 1), 0)
    valid = (c * T + row) >= META_PAD
    gpre = gate_ref[...] + gbias_ref[...]
    log_i = jnp.where(valid, gpre, -jnp.inf)
    log_f = jnp.where(valid, _log_sigmoid(gpre), 0.0)
    r_i = lax.broadcasted_iota(jnp.int32, (T, T), 0)
    c_i = lax.broadcasted_iota(jnp.int32, (T, T), 1)
    causal = c_i <= r_i
    tri = causal.astype(F32)
    bcum = jnp.dot(tri, log_f, preferred_element_type=F32, precision=lax.Precision.HIGHEST)
    bcum_t = bcum.T
    log_i_t = log_i.T
    e_r = lax.broadcasted_iota(jnp.int32, (T, 2 * NH * T), 0)
    e_c = lax.broadcasted_iota(jnp.int32, (T, 2 * NH * T), 1)
    spread = (e_r == e_c // T).astype(BF16)
    lane_g = lax.broadcasted_iota(jnp.int32, (1, T), 1)
    packed = jnp.where(lane_g < NH, gpre, bcum)
    on_lanes = sum(jnp.dot(part, spread, preferred_element_type=F32) for part in _split3_bf16(packed))
    yield

    lane = lax.broadcasted_iota(jnp.int32, (1, T), 1)
    heads = range(ML_HEADS)
    pairs = range(ML_HEADS // 2)
    halves = [(lane >= ML_DQK * e) & (lane < ML_DQK * (e + 1)) for e in range(2)]
    k_pairs = [k_all[:, 128 * p:128 * (p + 1)] for p in pairs]
    k_bf = [k.astype(BF16) for k in k_pairs]
    kt_bf = [k.T.astype(BF16) for k in k_pairs]
    ctn_old = [ctn_ref[p] for p in pairs]
    ctn_bf = [s.astype(BF16) for s in ctn_old]
    qms = [jnp.where(halves[h % 2], q_all[:, 128 * (h // 2):128 * (h // 2 + 1)], 0.0).astype(BF16)
           for h in heads]
    vs = [v_ref[:, 128 * h:128 * (h + 1)] for h in heads]
    ones_bf = jnp.ones((T, T), BF16)

    qk = [lax.dot_general(qms[h], k_bf[h // 2], (((1,), (1,)), ((), ())), preferred_element_type=F32)
          for h in heads]
    q_state = [jnp.dot(qms[h], ctn_bf[h // 2], preferred_element_type=F32) for h in heads]
    yield

    s_mats, a_inters, floor_dens, wvs, a_olds, a_locs, m_news = [], [], [], [], [], [], []
    for h in heads:
        b_t = on_lanes[:, (NH + h) * T:(NH + h + 1) * T]
        ig_t = jnp.where(valid, on_lanes[:, h * T:(h + 1) * T], -jnp.inf)
        a_row = log_i_t[h:h + 1, :] - bcum_t[NH + h:NH + h + 1, :]
        m_prev = m_ref[h:h + 1, :]
        b_last = b_t[T - 1:T, :]
        dm = jnp.where(causal, b_t + a_row, -jnp.inf)
        inter = b_t + m_prev
        m_t = jnp.maximum(inter, jnp.max(dm, axis=1, keepdims=True))
        s_mats.append(qk[h] * jnp.exp(dm - m_t))
        a_inters.append(jnp.exp(inter - m_t))
        floor_dens.append(jnp.exp(-m_t))
        g_t = b_last + ig_t - b_t
        m_loc = jnp.max(g_t, axis=0, keepdims=True)
        w_t = jnp.exp(g_t - m_loc)
        m_new = jnp.maximum(b_last + m_prev, m_loc)
        a_olds.append(jnp.exp(b_last + m_prev - m_new))
        a_locs.append(jnp.exp(m_loc - m_new))
        m_news.append(m_new)
        wvs.append(jnp.concatenate([(w_t * vs[h].astype(F32)).astype(BF16), w_t.astype(BF16)], axis=1))
    yield

    intra = [jnp.dot(s_mats[h].astype(BF16), jnp.concatenate([vs[h], ones_bf], axis=1),
                     preferred_element_type=F32) for h in heads]
    ctn_loc = [jnp.dot(kt_bf[h // 2], wvs[h], preferred_element_type=F32) for h in heads]
    yield

    outs = []
    for h in heads:
        num = intra[h][:, :T] + a_inters[h] * q_state[h][:, :T]
        den = intra[h][:, T:] + a_inters[h] * q_state[h][:, T:]
        hm = num / jnp.maximum(jnp.abs(den), floor_dens[h])
        outs.append(hm * jax.nn.sigmoid(o_ref[:, 128 * h:128 * (h + 1)].astype(F32)))
    out_ref[...] = jnp.concatenate(outs, axis=1).astype(out_ref.dtype)

    for h in heads:
        p, e = divmod(h, 2)
        lo, hi = ML_DQK * e, ML_DQK * (e + 1)
        for half in (slice(0, T), slice(T, 2 * T)):
            ctn_ref[p, lo:hi, half] = (a_olds[h] * ctn_old[p][lo:hi, half]
                                       + a_locs[h] * ctn_loc[h][lo:hi, half])
        m_ref[h:h + 1, :] = m_news[h]


def _sb_tile_stages(q_ref, k_ref, v_ref, acc_ref, rest_ref, later_ones, tiles, *, assign):
    T = BLOCK
    scale = 1.0 / math.sqrt(HEAD_DIM)
    heads = range(SB_HEADS)
    cols = [slice(HEAD_DIM * h, HEAD_DIM * (h + 1)) for h in heads]
    kofs = lax.broadcasted_iota(jnp.int32, (1, T), 1)

    zs = {}
    for t, (start, _, (r0, r1)) in enumerate(tiles):
        for h in heads:
            zs[t, h] = lax.dot_general(q_ref[r0:r1, cols[h]], k_ref[pl.ds(start, T), cols[h]],
                                       (((1,), (1,)), ((), ())), preferred_element_type=F32) * scale
    yield
    his, los, log_betas = {}, {}, {}
    for t, (start, is_diag, (r0, r1)) in enumerate(tiles):
        live = (start + kofs) >= META_PAD
        if is_diag:
            live = live & (kofs < r0 + lax.broadcasted_iota(jnp.int32, (r1 - r0, T), 0))
        for h in heads:
            z = zs[t, h]
            sp = jnp.maximum(z, 0.0) + jnp.log(1.0 + jnp.exp(-jnp.abs(z)))
            l1 = jnp.where(live, -sp, 0.0)
            his[t, h] = l1.astype(BF16)
            los[t, h] = (l1 - his[t, h].astype(F32)).astype(BF16)
            log_betas[t, h] = jnp.where(live, z - sp, -jnp.inf)
    yield
    sums = {key: jnp.dot(his[key], later_ones, preferred_element_type=F32)
            + jnp.dot(los[key], later_ones, preferred_element_type=F32) for key in his}
    yield
    probs = {}
    for h in heads:
        rows0 = tiles[0][2]
        rest = None if assign else rest_ref[h, rows0[0]:rows0[1], :]
        for t, (_, _, (r0, r1)) in enumerate(tiles):
            lo, hi = r0 - rows0[0], r1 - rows0[0]
            log_a = log_betas[t, h] + sums[t, h][:, :T]
            new = sums[t, h][:, T:]
            if rest is not None:
                log_a = log_a + rest[lo:hi]
                new = new + rest[lo:hi]
                pieces = ([rest[:lo]] if lo else []) + [new] + ([rest[hi:]] if hi < rest.shape[0] else [])
                new = pieces[0] if len(pieces) == 1 else jnp.concatenate(pieces, axis=0)
            probs[t, h] = jnp.exp(log_a).astype(BF16)
            rest = new
        rest_ref[h, rows0[0]:rows0[1], :] = rest
    yield
    for t, (start, _, (r0, r1)) in enumerate(tiles):
        for h in heads:
            pv = jnp.dot(probs[t, h], v_ref[pl.ds(start, T), cols[h]], preferred_element_type=F32)
            if assign and t == 0:
                acc_ref[r0:r1, cols[h]] = pv
            else:
                acc_ref[r0:r1, cols[h]] += pv


def _run_interleaved(*streams):
    live = list(streams)
    while live:
        for stream in list(live):
            try:
                next(stream)
            except StopIteration:
                live.remove(stream)


def _mixers_kernel(qkprev_ref, qk_ref, mlv_ref, mlo_ref, gate_ref, convw_ref, gbias_ref,
                   sbq_ref, sbk_ref, sbv_ref, ml_out_ref, sb_out_ref,
                   ctn_ref, m_ref, acc_ref, rest_ref):
    i = pl.program_id(1)
    T = BLOCK

    @pl.when(i == 0)
    def _():
        ctn_ref[...] = jnp.zeros_like(ctn_ref)
        m_ref[...] = jnp.zeros_like(m_ref)

    r_w = lax.broadcasted_iota(jnp.int32, (T, 2 * T), 0)
    c_w = lax.broadcasted_iota(jnp.int32, (T, 2 * T), 1)
    later_ones = ((r_w > c_w) | (c_w >= T)).astype(BF16)
    sb_tiles = functools.partial(_sb_tile_stages, sbq_ref, sbk_ref, sbv_ref, acc_ref, rest_ref, later_ones)
    mlstm = functools.partial(_mlstm_stages, i, qkprev_ref, qk_ref, mlv_ref, mlo_ref, gate_ref, convw_ref,
                              gbias_ref, ml_out_ref, ctn_ref, m_ref)
    full, top, bottom = (0, T), (0, SB_TOP_ROWS), (SB_TOP_ROWS, T)
    start = lambda back: pl.multiple_of((i - back) * T, T)

    @pl.when(i == 0)
    def _():
        _run_interleaved(mlstm(), sb_tiles([(start(0), True, full)], assign=True))

    @pl.when(i == 1)
    def _():
        _run_interleaved(mlstm(), sb_tiles([(start(0), True, full), (start(1), False, full)], assign=True))

    @pl.when(i >= 2)
    def _():
        _run_interleaved(mlstm(), sb_tiles([(start(0), True, full), (start(1), False, full),
                                            (start(2), False, top)], assign=True))

    def sweep(rows, j0):
        def alive():
            worst = rest_ref[0, rows[0]:rows[1], :]
            for h in range(1, SB_HEADS):
                worst = jnp.maximum(worst, rest_ref[h, rows[0]:rows[1], :])
            return jnp.max(worst)

        def cond(carry):
            j, worst = carry
            return jnp.logical_and(j >= 0, worst > SB_DEAD_LOG)

        def body(carry):
            j, _ = carry
            _run_interleaved(sb_tiles([(pl.multiple_of(j * T, T), False, rows)], assign=False))
            return j - 1, alive()

        lax.while_loop(cond, body, (j0, alive()))

    worst = rest_ref[0]
    for h in range(1, SB_HEADS):
        worst = jnp.maximum(worst, rest_ref[h])

    @pl.when(jnp.logical_and(i >= 2, jnp.max(worst) > SB_DEAD_LOG))
    def _():
        sweep(bottom, i - 2)
        sweep(top, i - 3)

    sb_out_ref[...] = acc_ref[...].astype(sb_out_ref.dtype)


def _mixers(u, gates, conv_w, gate_bias, *, batch, seq_len):
    nb = seq_len // BLOCK
    u3 = u.reshape(batch, seq_len, U_WIDTH)
    g3 = gates.reshape(batch, seq_len, GATE_W)
    blk = lambda width, col: pl.BlockSpec((None, BLOCK, width), lambda b, i: (b, i, col))
    whole = lambda col: pl.BlockSpec((None, seq_len, SB_WIDTH), lambda b, i: (b, 0, col),
                                     pipeline_mode=pl.Buffered(1))
    ml, sb = pl.pallas_call(
        _mixers_kernel,
        grid=(batch, nb),
        in_specs=[
            pl.BlockSpec((None, BLOCK, SEG), lambda b, i: (b, jnp.maximum(i - 1, 0), COL_QK // SEG)),
            blk(SEG, COL_QK // SEG),
            blk(SEG, COL_MLV // SEG),
            blk(SEG, COL_MLO // SEG),
            blk(GATE_W, 0),
            pl.BlockSpec((ML_CONV, 2 * ML_QK_WIDTH), lambda b, i: (0, 0)),
            pl.BlockSpec((1, GATE_W), lambda b, i: (0, 0)),
            blk(SB_WIDTH, COL_SBQ // SEG),
            whole(COL_SBK // SEG),
            whole(COL_SBV // SEG),
        ],
        out_specs=[blk(ML_WIDTH, 0), blk(SB_WIDTH, 0)],
        out_shape=[jax.ShapeDtypeStruct((batch, seq_len, ML_WIDTH), BF16),
                   jax.ShapeDtypeStruct((batch, seq_len, SB_WIDTH), BF16)],
        scratch_shapes=[
            pltpu.VMEM((ML_HEADS // 2, 128, 256), F32),
            pltpu.VMEM((8, 128), F32),
            pltpu.VMEM((BLOCK, SB_WIDTH), F32),
            pltpu.VMEM((SB_HEADS, BLOCK, BLOCK), F32),
        ],
        compiler_params=pltpu.CompilerParams(
            dimension_semantics=("arbitrary", "arbitrary"),
            vmem_limit_bytes=V7X_VMEM_LIMIT),
        name="mixers",
    )(u3, u3, u3, u3, g3, conv_w, gate_bias, u3, u3, u3)
    return ml.reshape(batch * seq_len, ML_WIDTH), sb.reshape(batch * seq_len, SB_WIDTH)


def _out_proj_kernel(ml_ref, sb_ref, pool_ref, halo_ref, h_ref, wo_ref, pw_ref, ps_ref, g_ref,
                     out_ref, ext_ref, *, tm, seq_len):
    i = pl.program_id(0)
    base = lax.rem(i * tm, seq_len)
    pos = base + lax.broadcasted_iota(jnp.int32, (tm, 1), 0)
    validf = (pos >= META_PAD).astype(F32)

    ext_ref[0:POOL_HALO, :] = jnp.where(base == 0, 0.0, halo_ref[...].astype(F32))
    ext_ref[POOL_HALO:POOL_HALO + tm, :] = pool_ref[...].astype(F32) * validf

    n_chunks = len(POOL_WINDOWS)
    cw = D_MODEL // n_chunks
    partial = []
    pooled = []
    for g, w in enumerate(POOL_WINDOWS):
        cols = slice(g * cw, (g + 1) * cw)
        partial.append(
            jnp.dot(ml_ref[...], wo_ref[0:ML_WIDTH, cols], preferred_element_type=F32)
            + jnp.dot(sb_ref[...], wo_ref[ML_WIDTH:ML_WIDTH + SB_WIDTH, cols], preferred_element_type=F32))
        lo, hi = g * POOL_CH, (g + 1) * POOL_CH
        uf = ext_ref[POOL_HALO:POOL_HALO + tm, lo:hi]
        s = uf
        for d in range(1, w):
            s = s + ext_ref[POOL_HALO - d:POOL_HALO - d + tm, lo:hi]
        cnt = jnp.clip(pos - (META_PAD - 1), 0, w).astype(F32)
        centred = s / jnp.maximum(cnt, 1.0) - uf
        yg = jnp.dot(centred.astype(BF16), pw_ref[g], preferred_element_type=F32) * ps_ref[g:g + 1, :]
        pooled.append(yg.astype(BF16))
    pool_out = jnp.concatenate(pooled, axis=1)

    y = jnp.concatenate(
        [partial[c] + jnp.dot(pool_out, wo_ref[ML_WIDTH + SB_WIDTH:, c * cw:(c + 1) * cw],
                              preferred_element_type=F32) for c in range(n_chunks)], axis=1)
    out_ref[...] = h_ref[...] + _rms_scale(y, g_ref[...])


def _out_proj(ml, sb, u, h, w_out, pool_w, pool_scale, g_post, layer, *, seq_len, tm):
    m = h.shape[0]
    halo_blocks = tm // POOL_HALO
    return pl.pallas_call(
        functools.partial(_out_proj_kernel, tm=tm, seq_len=seq_len),
        grid=(m // tm,),
        in_specs=[
            pl.BlockSpec((tm, ML_WIDTH), lambda i: (i, 0)),
            pl.BlockSpec((tm, SB_WIDTH), lambda i: (i, 0)),
            pl.BlockSpec((tm, POOL_WIDTH), lambda i: (i, COL_POOL // POOL_WIDTH)),
            pl.BlockSpec((POOL_HALO, POOL_WIDTH),
                         lambda i: (jnp.maximum(i * halo_blocks - 1, 0), COL_POOL // POOL_WIDTH)),
            pl.BlockSpec((tm, D_MODEL), lambda i: (i, 0)),
            pl.BlockSpec((None, D_MODEL, D_MODEL), lambda i: (layer, 0, 0), pipeline_mode=pl.Buffered(1)),
            pl.BlockSpec((None, len(POOL_WINDOWS), POOL_CH, POOL_CH), lambda i: (layer, 0, 0, 0)),
            pl.BlockSpec((len(POOL_WINDOWS), POOL_CH), lambda i: (0, 0)),
            pl.BlockSpec((1, D_MODEL), lambda i: (0, 0)),
        ],
        out_specs=pl.BlockSpec((tm, D_MODEL), lambda i: (i, 0)),
        out_shape=jax.ShapeDtypeStruct((m, D_MODEL), F32),
        scratch_shapes=[pltpu.VMEM((POOL_HALO + tm, POOL_WIDTH), F32)],
        compiler_params=pltpu.CompilerParams(
            dimension_semantics=("arbitrary",),
            vmem_limit_bytes=V7X_VMEM_LIMIT),
        name="out_proj",
    )(ml, sb, u, u, h, w_out, pool_w, pool_scale, g_post)


def _ffn_kernel(h_ref, gpre_ref, wg_ref, wu_ref, wd_ref, gpost_ref, out_ref, xn_ref):
    f = pl.program_id(1)

    @pl.when(f == 0)
    def _():
        xn_ref[...] = _rms_scale(h_ref[...], gpre_ref[...]).astype(BF16)
        out_ref[...] = jnp.zeros_like(out_ref)

    xn = xn_ref[...]
    gate = jnp.dot(xn, wg_ref[...], preferred_element_type=F32)
    up = jnp.dot(xn, wu_ref[...], preferred_element_type=F32)
    act = (gate * jax.nn.sigmoid(gate) * up).astype(BF16)
    out_ref[...] += jnp.dot(act, wd_ref[...], preferred_element_type=F32)

    @pl.when(f == pl.num_programs(1) - 1)
    def _():
        out_ref[...] = h_ref[...] + _rms_scale(out_ref[...], gpost_ref[...])


def _ffn(h, g_pre, w_gate_up, w_down, g_post, layer, *, tm, tf):
    m = h.shape[0]
    nf = FFN_HIDDEN // tf
    return pl.pallas_call(
        _ffn_kernel,
        grid=(m // tm, nf),
        in_specs=[
            pl.BlockSpec((tm, D_MODEL), lambda i, f: (i, 0)),
            pl.BlockSpec((1, D_MODEL), lambda i, f: (0, 0)),
            pl.BlockSpec((None, D_MODEL, tf), lambda i, f: (layer, 0, f)),
            pl.BlockSpec((None, D_MODEL, tf), lambda i, f: (layer, 0, nf + f)),
            pl.BlockSpec((None, tf, D_MODEL), lambda i, f: (layer, f, 0)),
            pl.BlockSpec((1, D_MODEL), lambda i, f: (0, 0)),
        ],
        out_specs=pl.BlockSpec((tm, D_MODEL), lambda i, f: (i, 0)),
        out_shape=jax.ShapeDtypeStruct((m, D_MODEL), F32),
        scratch_shapes=[pltpu.VMEM((tm, D_MODEL), BF16)],
        compiler_params=pltpu.CompilerParams(
            dimension_semantics=("arbitrary", "arbitrary"),
            vmem_limit_bytes=V7X_VMEM_LIMIT),
        name="ffn",
    )(h, g_pre, w_gate_up, w_gate_up, w_down, g_post)


def _row_tile(seq_len, target):
    best = 16
    for t in range(16, target + 1, 16):
        if seq_len % t == 0:
            best = t
    return best


def kernel(x, meta_tokens, w_in, ml_conv_w, ml_igate_b, ml_fgate_b, pool_w, pool_scale, w_out,
           g_mix_pre, g_mix_post, g_ffn_pre, g_ffn_post, w_gate_up, w_down):
    batch, seq, d = x.shape
    depth = w_in.shape[0]
    seq_len = BLOCK + seq
    meta = jnp.broadcast_to(meta_tokens[None].astype(x.dtype), (batch, N_META, d))
    h = jnp.concatenate([jnp.zeros((batch, META_PAD, d), x.dtype), meta, x], axis=1)
    h = h.reshape(batch * seq_len, d)

    w_in_r = _w_in_prep(w_in)
    w_out_bf, pool_w_bf = w_out.astype(BF16), pool_w.astype(BF16)
    w_gate_up_bf, w_down_bf = w_gate_up.astype(BF16), w_down.astype(BF16)
    tm_mm = _row_tile(seq_len, 640)
    tm_out = _row_tile(seq_len, 640)
    tm_ffn = _row_tile(seq_len, 832)
    for l in range(depth):
        gate_bias = jnp.concatenate(
            [ml_igate_b[l], ml_fgate_b[l], jnp.zeros((GATE_W - N_GATES,), F32)])[None, :]
        u, gates = _in_proj(h, g_mix_pre[l][None, :], w_in_r, l, seq_len=seq_len, tm=tm_mm)
        ml, sb = _mixers(u, gates, ml_conv_w[l], gate_bias, batch=batch, seq_len=seq_len)
        h = _out_proj(ml, sb, u, h, w_out_bf, pool_w_bf, pool_scale[l], g_mix_post[l][None, :], l,
                      seq_len=seq_len, tm=tm_out)
        h = _ffn(h, g_ffn_pre[l][None, :], w_gate_up_bf, w_down_bf, g_ffn_post[l][None, :], l,
                 tm=tm_ffn, tf=512)
    return h.reshape(batch, seq_len, d)[:, BLOCK:, :]
```

```python
import functools
import math

import jax
import jax.numpy as jnp
from jax import lax
from jax.experimental import pallas as pl
from jax.experimental.pallas import tpu as pltpu

F32 = jnp.float32
BF16 = jnp.bfloat16

D_MODEL = 2048
N_META = 16
BLOCK = 128
META_PAD = BLOCK - N_META
EPS = 1e-6

HEAD_DIM = 128
ML_HEADS = 6
ML_DQK = 64
ML_QK_WIDTH = ML_HEADS * ML_DQK
ML_WIDTH = ML_HEADS * HEAD_DIM
ML_CONV = 4
SB_HEADS = 6
SB_WIDTH = SB_HEADS * HEAD_DIM
POOL_WINDOWS = (2, 4, 8, 16)
POOL_CH = 128
POOL_WIDTH = POOL_CH * len(POOL_WINDOWS)
POOL_HALO = 16
FFN_HIDDEN = 5632
IN_SIZES = (ML_QK_WIDTH, ML_QK_WIDTH, ML_WIDTH, ML_WIDTH, ML_HEADS, ML_HEADS,
            SB_WIDTH, SB_WIDTH, SB_WIDTH, POOL_WIDTH)
IN_WIDTH = sum(IN_SIZES)
IN_GATE_OFF = 2 * ML_QK_WIDTH + 2 * ML_WIDTH
N_GATES = 2 * ML_HEADS

SEG = 768
COL_QK, COL_MLV, COL_MLO, COL_SBQ, COL_SBK, COL_SBV = (i * SEG for i in range(6))
COL_POOL = 6 * SEG
U_WIDTH = COL_POOL + POOL_WIDTH
GATE_W = 128
N_IN = U_WIDTH + GATE_W
IN_CHUNKS = tuple((i * SEG, SEG) for i in range(6)) + ((COL_POOL, POOL_WIDTH),)

V7X_VMEM_LIMIT = 56 * 1024 * 1024

SB_DEAD_LOG = -104.0
SB_TOP_ROWS = 32


def _rms_scale(x, g):
    ms = jnp.mean(x * x, axis=-1, keepdims=True)
    return x * lax.rsqrt(ms + EPS) * g


def _w_in_prep_kernel(w_ref, out_ref):
    x = w_ref[...]
    out_ref[:, 0:IN_GATE_OFF] = x[:, 0:IN_GATE_OFF].astype(BF16)
    out_ref[:, IN_GATE_OFF:U_WIDTH] = x[:, IN_GATE_OFF + N_GATES:IN_WIDTH].astype(BF16)
    lane = lax.broadcasted_iota(jnp.int32, (1, GATE_W), 1)
    gates = jnp.where(lane < N_GATES, x[:, IN_GATE_OFF:IN_GATE_OFF + GATE_W], 0.0)
    out_ref[:, U_WIDTH:N_IN] = gates.astype(BF16)


def _w_in_prep(w_in, *, tr=256):
    depth = w_in.shape[0]
    return pl.pallas_call(
        _w_in_prep_kernel,
        grid=(depth, D_MODEL // tr),
        in_specs=[pl.BlockSpec((None, tr, IN_WIDTH), lambda l, r: (l, r, 0))],
        out_specs=pl.BlockSpec((None, tr, N_IN), lambda l, r: (l, r, 0)),
        out_shape=jax.ShapeDtypeStruct((depth, D_MODEL, N_IN), BF16),
        compiler_params=pltpu.CompilerParams(
            dimension_semantics=("arbitrary", "arbitrary"),
            vmem_limit_bytes=V7X_VMEM_LIMIT),
        name="w_in_prep",
    )(w_in)


def _in_proj_kernel(h_ref, g_ref, w_ref, u_ref, gate_ref, xn_ref, *, tm, seq_len):
    i = pl.program_id(0)
    xn_ref[...] = _rms_scale(h_ref[...], g_ref[...]).astype(BF16)
    pos = lax.rem(i * tm, seq_len) + lax.broadcasted_iota(jnp.int32, (tm, 1), 0)
    validf = (pos >= META_PAD).astype(F32)
    for c0, cw in IN_CHUNKS:
        acc = jnp.dot(xn_ref[...], w_ref[:, c0:c0 + cw], preferred_element_type=F32)
        u_ref[:, c0:c0 + cw] = (acc * validf).astype(BF16)
    acc = jnp.dot(xn_ref[...], w_ref[:, U_WIDTH:N_IN], preferred_element_type=F32)
    gate_ref[...] = acc * validf


def _in_proj(h, g, w_all, layer, *, seq_len, tm):
    m = h.shape[0]
    return pl.pallas_call(
        functools.partial(_in_proj_kernel, tm=tm, seq_len=seq_len),
        grid=(m // tm,),
        in_specs=[
            pl.BlockSpec((tm, D_MODEL), lambda i: (i, 0)),
            pl.BlockSpec((1, D_MODEL), lambda i: (0, 0)),
            pl.BlockSpec((None, D_MODEL, N_IN), lambda i: (layer, 0, 0), pipeline_mode=pl.Buffered(1)),
        ],
        out_specs=[
            pl.BlockSpec((tm, U_WIDTH), lambda i: (i, 0)),
            pl.BlockSpec((tm, GATE_W), lambda i: (i, 0)),
        ],
        out_shape=[jax.ShapeDtypeStruct((m, U_WIDTH), BF16), jax.ShapeDtypeStruct((m, GATE_W), F32)],
        scratch_shapes=[pltpu.VMEM((tm, D_MODEL), BF16)],
        compiler_params=pltpu.CompilerParams(
            dimension_semantics=("arbitrary",),
            vmem_limit_bytes=V7X_VMEM_LIMIT),
        name="in_proj",
    )(h, g, w_all)


def _log_sigmoid(x):
    return jnp.minimum(x, 0.0) - jnp.log(1.0 + jnp.exp(-jnp.abs(x)))


def _split3_bf16(x):
    hi = x.astype(BF16)
    r1 = x - hi.astype(F32)
    mid = r1.astype(BF16)
    lo = (r1 - mid.astype(F32)).astype(BF16)
    return hi, mid, lo


def _mlstm_stages(c, qkprev_ref, qk_ref, v_ref, o_ref, gate_ref, convw_ref, gbias_ref, out_ref,
                  ctn_ref, m_ref):
    T = BLOCK
    NH = ML_HEADS

    cur = qk_ref[...]
    prev = jnp.where(c == 0, jnp.zeros_like(cur), qkprev_ref[...])
    both = jnp.concatenate([prev, cur], axis=0)
    rr = lax.broadcasted_iota(jnp.int32, ((ML_CONV - 1) * T, 2 * T), 0)
    cc = lax.broadcasted_iota(jnp.int32, ((ML_CONV - 1) * T, 2 * T), 1)
    delay = (ML_CONV - 1) - rr // T
    shift = (cc == T + rr % T - delay).astype(BF16)
    delayed = jnp.dot(shift, both, preferred_element_type=F32)
    yield
    conv = convw_ref[ML_CONV - 1:ML_CONV, :] * cur.astype(F32)
    for j in range(ML_CONV - 1):
        conv = conv + convw_ref[j:j + 1, :] * delayed[j * T:(j + 1) * T, :]
    qk = conv * jax.nn.sigmoid(conv)
    q_all = qk[:, :ML_QK_WIDTH]
    k_all = qk[:, ML_QK_WIDTH:] * (1.0 / math.sqrt(ML_DQK))

    row = lax.broadcasted_iota(jnp.int32, (T, 1), 0)
    valid = (c * T + row) >= META_PAD
    gpre = gate_ref[...] + gbias_ref[...]
    log_i = jnp.where(valid, gpre, -jnp.inf)
    log_f = jnp.where(valid, _log_sigmoid(gpre), 0.0)
    r_i = lax.broadcasted_iota(jnp.int32, (T, T), 0)
    c_i = lax.broadcasted_iota(jnp.int32, (T, T), 1)
    causal = c_i <= r_i
    tri = causal.astype(F32)
    bcum = jnp.dot(tri, log_f, preferred_element_type=F32, precision=lax.Precision.HIGHEST)
    bcum_t = bcum.T
    log_i_t = log_i.T
    e_r = lax.broadcasted_iota(jnp.int32, (T, 2 * NH * T), 0)
    e_c = lax.broadcasted_iota(jnp.int32, (T, 2 * NH * T), 1)
    spread = (e_r == e_c // T).astype(BF16)
    lane_g = lax.broadcasted_iota(jnp.int32, (1, T), 1)
    packed = jnp.where(lane_g < NH, gpre, bcum)
    on_lanes = sum(jnp.dot(part, spread, preferred_element_type=F32) for part in _split3_bf16(packed))
    yield

    lane = lax.broadcasted_iota(jnp.int32, (1, T), 1)
    heads = range(ML_HEADS)
    pairs = range(ML_HEADS // 2)
    halves = [(lane >= ML_DQK * e) & (lane < ML_DQK * (e + 1)) for e in range(2)]
    k_pairs = [k_all[:, 128 * p:128 * (p + 1)] for p in pairs]
    k_bf = [k.astype(BF16) for k in k_pairs]
    kt_bf = [k.T.astype(BF16) for k in k_pairs]
    ctn_old = [ctn_ref[p] for p in pairs]
    ctn_bf = [s.astype(BF16) for s in ctn_old]
    qms = [jnp.where(halves[h % 2], q_all[:, 128 * (h // 2):128 * (h // 2 + 1)], 0.0).astype(BF16)
           for h in heads]
    vs = [v_ref[:, 128 * h:128 * (h + 1)] for h in heads]
    ones_bf = jnp.ones((T, T), BF16)

    qk = [lax.dot_general(qms[h], k_bf[h // 2], (((1,), (1,)), ((), ())), preferred_element_type=F32)
          for h in heads]
    q_state = [jnp.dot(qms[h], ctn_bf[h // 2], preferred_element_type=F32) for h in heads]
    yield

    s_mats, a_inters, floor_dens, wvs, a_olds, a_locs, m_news = [], [], [], [], [], [], []
    for h in heads:
        b_t = on_lanes[:, (NH + h) * T:(NH + h + 1) * T]
        ig_t = jnp.where(valid, on_lanes[:, h * T:(h + 1) * T], -jnp.inf)
        a_row = log_i_t[h:h + 1, :] - bcum_t[NH + h:NH + h + 1, :]
        m_prev = m_ref[h:h + 1, :]
        b_last = b_t[T - 1:T, :]
        dm = jnp.where(causal, b_t + a_row, -jnp.inf)
        inter = b_t + m_prev
        m_t = jnp.maximum(inter, jnp.max(dm, axis=1, keepdims=True))
        s_mats.append(qk[h] * jnp.exp(dm - m_t))
        a_inters.append(jnp.exp(inter - m_t))
        floor_dens.append(jnp.exp(-m_t))
        g_t = b_last + ig_t - b_t
        m_loc = jnp.max(g_t, axis=0, keepdims=True)
        w_t = jnp.exp(g_t - m_loc)
        m_new = jnp.maximum(b_last + m_prev, m_loc)
        a_olds.append(jnp.exp(b_last + m_prev - m_new))
        a_locs.append(jnp.exp(m_loc - m_new))
        m_news.append(m_new)
        wvs.append(jnp.concatenate([(w_t * vs[h].astype(F32)).astype(BF16), w_t.astype(BF16)], axis=1))
    yield

    intra = [jnp.dot(s_mats[h].astype(BF16), jnp.concatenate([vs[h], ones_bf], axis=1),
                     preferred_element_type=F32) for h in heads]
    ctn_loc = [jnp.dot(kt_bf[h // 2], wvs[h], preferred_element_type=F32) for h in heads]
    yield

    outs = []
    for h in heads:
        num = intra[h][:, :T] + a_inters[h] * q_state[h][:, :T]
        den = intra[h][:, T:] + a_inters[h] * q_state[h][:, T:]
        hm = num / jnp.maximum(jnp.abs(den), floor_dens[h])
        outs.append(hm * jax.nn.sigmoid(o_ref[:, 128 * h:128 * (h + 1)].astype(F32)))
    out_ref[...] = jnp.concatenate(outs, axis=1).astype(out_ref.dtype)

    for h in heads:
        p, e = divmod(h, 2)
        lo, hi = ML_DQK * e, ML_DQK * (e + 1)
        for half in (slice(0, T), slice(T, 2 * T)):
            ctn_ref[p, lo:hi, half] = (a_olds[h] * ctn_old[p][lo:hi, half]
                                       + a_locs[h] * ctn_loc[h][lo:hi, half])
        m_ref[h:h + 1, :] = m_news[h]


def _sb_tile_stages(q_ref, k_ref, v_ref, acc_ref, rest_ref, later_ones, tiles, *, assign, alive_ref=None):
    T = BLOCK
    scale = 1.0 / math.sqrt(HEAD_DIM)
    heads = range(SB_HEADS)
    cols = [slice(HEAD_DIM * h, HEAD_DIM * (h + 1)) for h in heads]
    kofs = lax.broadcasted_iota(jnp.int32, (1, T), 1)

    zs = {}
    for t, (start, _, (r0, r1)) in enumerate(tiles):
        for h in heads:
            zs[t, h] = lax.dot_general(q_ref[r0:r1, cols[h]], k_ref[pl.ds(start, T), cols[h]],
                                       (((1,), (1,)), ((), ())), preferred_element_type=F32) * scale
    yield
    his, los, log_betas = {}, {}, {}
    for t, (start, is_diag, (r0, r1)) in enumerate(tiles):
        live = (start + kofs) >= META_PAD
        if is_diag:
            live = live & (kofs < r0 + lax.broadcasted_iota(jnp.int32, (r1 - r0, T), 0))
        for h in heads:
            z = zs[t, h]
            sp = jnp.maximum(z, 0.0) + jnp.log(1.0 + jnp.exp(-jnp.abs(z)))
            l1 = jnp.where(live, -sp, 0.0)
            his[t, h] = l1.astype(BF16)
            los[t, h] = (l1 - his[t, h].astype(F32)).astype(BF16)
            log_betas[t, h] = jnp.where(live, z - sp, -jnp.inf)
    yield
    sums = {key: jnp.dot(his[key], later_ones, preferred_element_type=F32)
            + jnp.dot(los[key], later_ones, preferred_element_type=F32) for key in his}
    yield
    probs = {}
    worst = None
    for h in heads:
        rows0 = tiles[0][2]
        rest = None if assign else rest_ref[h, rows0[0]:rows0[1], :]
        for t, (_, _, (r0, r1)) in enumerate(tiles):
            lo, hi = r0 - rows0[0], r1 - rows0[0]
            log_a = log_betas[t, h] + sums[t, h][:, :T]
            new = sums[t, h][:, T:]
            if rest is not None:
                log_a = log_a + rest[lo:hi]
                new = new + rest[lo:hi]
                pieces = ([rest[:lo]] if lo else []) + [new] + ([rest[hi:]] if hi < rest.shape[0] else [])
                new = pieces[0] if len(pieces) == 1 else jnp.concatenate(pieces, axis=0)
            probs[t, h] = jnp.exp(log_a).astype(BF16)
            rest = new
        rest_ref[h, rows0[0]:rows0[1], :] = rest
        worst = rest if worst is None else jnp.maximum(worst, rest)
    if alive_ref is not None:
        alive_ref[0] = jnp.max(worst)
    yield
    for t, (start, _, (r0, r1)) in enumerate(tiles):
        for h in heads:
            pv = jnp.dot(probs[t, h], v_ref[pl.ds(start, T), cols[h]], preferred_element_type=F32)
            if assign and t == 0:
                acc_ref[r0:r1, cols[h]] = pv
            else:
                acc_ref[r0:r1, cols[h]] += pv


def _run_interleaved(*streams):
    live = list(streams)
    while live:
        for stream in list(live):
            try:
                next(stream)
            except StopIteration:
                live.remove(stream)


def _mixers_kernel(qkprev_ref, qk_ref, mlv_ref, mlo_ref, gate_ref, convw_ref, gbias_ref,
                   sbq_ref, sbk_ref, sbv_ref, later_ones_ref, ml_out_ref, sb_out_ref,
                   ctn_ref, m_ref, acc_ref, rest_ref, alive_ref):
    i = pl.program_id(1)
    T = BLOCK

    @pl.when(i == 0)
    def _():
        ctn_ref[...] = jnp.zeros_like(ctn_ref)
        m_ref[...] = jnp.zeros_like(m_ref)

    later_ones = later_ones_ref[...]
    sb_tiles = functools.partial(_sb_tile_stages, sbq_ref, sbk_ref, sbv_ref, acc_ref, rest_ref, later_ones)
    mlstm = functools.partial(_mlstm_stages, i, qkprev_ref, qk_ref, mlv_ref, mlo_ref, gate_ref, convw_ref,
                              gbias_ref, ml_out_ref, ctn_ref, m_ref)
    full, top, bottom = (0, T), (0, SB_TOP_ROWS), (SB_TOP_ROWS, T)
    start = lambda back: pl.multiple_of((i - back) * T, T)

    @pl.when(i == 0)
    def _():
        _run_interleaved(mlstm(), sb_tiles([(start(0), True, full)], assign=True, alive_ref=alive_ref))

    @pl.when(i == 1)
    def _():
        _run_interleaved(mlstm(), sb_tiles([(start(0), True, full), (start(1), False, full)], assign=True, alive_ref=alive_ref))

    @pl.when(i >= 2)
    def _():
        _run_interleaved(mlstm(), sb_tiles([(start(0), True, full), (start(1), False, full),
                                            (start(2), False, top)], assign=True, alive_ref=alive_ref))

    def sweep(rows, j0):
        def alive():
            worst = rest_ref[0, rows[0]:rows[1], :]
            for h in range(1, SB_HEADS):
                worst = jnp.maximum(worst, rest_ref[h, rows[0]:rows[1], :])
            return jnp.max(worst)

        def cond(carry):
            j, worst = carry
            return jnp.logical_and(j >= 0, worst > SB_DEAD_LOG)

        def body(carry):
            j, _ = carry
            _run_interleaved(sb_tiles([(pl.multiple_of(j * T, T), False, rows)], assign=False))
            return j - 1, alive()

        lax.while_loop(cond, body, (j0, alive()))

    @pl.when(jnp.logical_and(i >= 2, alive_ref[0] > SB_DEAD_LOG))
    def _():
        sweep(bottom, i - 2)
        sweep(top, i - 3)

    sb_out_ref[...] = acc_ref[...].astype(sb_out_ref.dtype)


def _mixers(u, gates, conv_w, gate_bias, *, batch, seq_len):
    nb = seq_len // BLOCK
    u3 = u.reshape(batch, seq_len, U_WIDTH)
    g3 = gates.reshape(batch, seq_len, GATE_W)
    blk = lambda width, col: pl.BlockSpec((None, BLOCK, width), lambda b, i: (b, i, col))
    whole = lambda col: pl.BlockSpec((None, seq_len, SB_WIDTH), lambda b, i: (b, 0, col),
                                     pipeline_mode=pl.Buffered(1))
    r_w = lax.broadcasted_iota(jnp.int32, (BLOCK, 2 * BLOCK), 0)
    c_w = lax.broadcasted_iota(jnp.int32, (BLOCK, 2 * BLOCK), 1)
    later_ones = ((r_w > c_w) | (c_w >= BLOCK)).astype(BF16)
    ml, sb = pl.pallas_call(
        _mixers_kernel,
        grid=(batch, nb),
        in_specs=[
            pl.BlockSpec((None, BLOCK, SEG), lambda b, i: (b, jnp.maximum(i - 1, 0), COL_QK // SEG)),
            blk(SEG, COL_QK // SEG),
            blk(SEG, COL_MLV // SEG),
            blk(SEG, COL_MLO // SEG),
            blk(GATE_W, 0),
            pl.BlockSpec((ML_CONV, 2 * ML_QK_WIDTH), lambda b, i: (0, 0)),
            pl.BlockSpec((1, GATE_W), lambda b, i: (0, 0)),
            blk(SB_WIDTH, COL_SBQ // SEG),
            whole(COL_SBK // SEG),
            whole(COL_SBV // SEG),
            pl.BlockSpec((BLOCK, 2 * BLOCK), lambda b, i: (0, 0)),
        ],
        out_specs=[blk(ML_WIDTH, 0), blk(SB_WIDTH, 0)],
        out_shape=[jax.ShapeDtypeStruct((batch, seq_len, ML_WIDTH), BF16),
                   jax.ShapeDtypeStruct((batch, seq_len, SB_WIDTH), BF16)],
        scratch_shapes=[
            pltpu.VMEM((ML_HEADS // 2, 128, 256), F32),
            pltpu.VMEM((8, 128), F32),
            pltpu.VMEM((BLOCK, SB_WIDTH), F32),
            pltpu.VMEM((SB_HEADS, BLOCK, BLOCK), F32),
            pltpu.SMEM((1,), F32),
        ],
        compiler_params=pltpu.CompilerParams(
            dimension_semantics=("arbitrary", "arbitrary"),
            vmem_limit_bytes=V7X_VMEM_LIMIT),
        name="mixers",
    )(u3, u3, u3, u3, g3, conv_w, gate_bias, u3, u3, u3, later_ones)
    return ml.reshape(batch * seq_len, ML_WIDTH), sb.reshape(batch * seq_len, SB_WIDTH)


def _out_proj_kernel(ml_ref, sb_ref, pool_ref, halo_ref, h_ref, wo_ref, pw_ref, ps_ref, g_ref,
                     out_ref, ext_ref, *, tm, seq_len):
    i = pl.program_id(0)
    base = lax.rem(i * tm, seq_len)
    pos = base + lax.broadcasted_iota(jnp.int32, (tm, 1), 0)
    validf = (pos >= META_PAD).astype(F32)

    ext_ref[0:POOL_HALO, :] = jnp.where(base == 0, 0.0, halo_ref[...].astype(F32))
    ext_ref[POOL_HALO:POOL_HALO + tm, :] = pool_ref[...].astype(F32) * validf

    n_chunks = len(POOL_WINDOWS)
    cw = D_MODEL // n_chunks
    partial = []
    pooled = []
    for g, w in enumerate(POOL_WINDOWS):
        cols = slice(g * cw, (g + 1) * cw)
        partial.append(
            jnp.dot(ml_ref[...], wo_ref[0:ML_WIDTH, cols], preferred_element_type=F32)
            + jnp.dot(sb_ref[...], wo_ref[ML_WIDTH:ML_WIDTH + SB_WIDTH, cols], preferred_element_type=F32))
        lo, hi = g * POOL_CH, (g + 1) * POOL_CH
        uf = ext_ref[POOL_HALO:POOL_HALO + tm, lo:hi]
        s = uf
        for d in range(1, w):
            s = s + ext_ref[POOL_HALO - d:POOL_HALO - d + tm, lo:hi]
        cnt = jnp.clip(pos - (META_PAD - 1), 0, w).astype(F32)
        centred = s / jnp.maximum(cnt, 1.0) - uf
        yg = jnp.dot(centred.astype(BF16), pw_ref[g], preferred_element_type=F32) * ps_ref[g:g + 1, :]
        pooled.append(yg.astype(BF16))
    pool_out = jnp.concatenate(pooled, axis=1)

    y = jnp.concatenate(
        [partial[c] + jnp.dot(pool_out, wo_ref[ML_WIDTH + SB_WIDTH:, c * cw:(c + 1) * cw],
                              preferred_element_type=F32) for c in range(n_chunks)], axis=1)
    out_ref[...] = h_ref[...] + _rms_scale(y, g_ref[...])


def _out_proj(ml, sb, u, h, w_out, pool_w, pool_scale, g_post, layer, *, seq_len, tm):
    m = h.shape[0]
    halo_blocks = tm // POOL_HALO
    return pl.pallas_call(
        functools.partial(_out_proj_kernel, tm=tm, seq_len=seq_len),
        grid=(m // tm,),
        in_specs=[
            pl.BlockSpec((tm, ML_WIDTH), lambda i: (i, 0)),
            pl.BlockSpec((tm, SB_WIDTH), lambda i: (i, 0)),
            pl.BlockSpec((tm, POOL_WIDTH), lambda i: (i, COL_POOL // POOL_WIDTH)),
            pl.BlockSpec((POOL_HALO, POOL_WIDTH),
                         lambda i: (jnp.maximum(i * halo_blocks - 1, 0), COL_POOL // POOL_WIDTH)),
            pl.BlockSpec((tm, D_MODEL), lambda i: (i, 0)),
            pl.BlockSpec((None, D_MODEL, D_MODEL), lambda i: (layer, 0, 0), pipeline_mode=pl.Buffered(1)),
            pl.BlockSpec((None, len(POOL_WINDOWS), POOL_CH, POOL_CH), lambda i: (layer, 0, 0, 0)),
            pl.BlockSpec((len(POOL_WINDOWS), POOL_CH), lambda i: (0, 0)),
            pl.BlockSpec((1, D_MODEL), lambda i: (0, 0)),
        ],
        out_specs=pl.BlockSpec((tm, D_MODEL), lambda i: (i, 0)),
        out_shape=jax.ShapeDtypeStruct((m, D_MODEL), F32),
        scratch_shapes=[pltpu.VMEM((POOL_HALO + tm, POOL_WIDTH), F32)],
        compiler_params=pltpu.CompilerParams(
            dimension_semantics=("arbitrary",),
            vmem_limit_bytes=V7X_VMEM_LIMIT),
        name="out_proj",
    )(ml, sb, u, u, h, w_out, pool_w, pool_scale, g_post)


def _ffn_kernel(h_ref, gpre_ref, wg_ref, wu_ref, wd_ref, gpost_ref, out_ref, xn_ref):
    f = pl.program_id(1)

    @pl.when(f == 0)
    def _():
        xn_ref[...] = _rms_scale(h_ref[...], gpre_ref[...]).astype(BF16)
        out_ref[...] = jnp.zeros_like(out_ref)

    xn = xn_ref[...]
    gate = jnp.dot(xn, wg_ref[...], preferred_element_type=F32)
    up = jnp.dot(xn, wu_ref[...], preferred_element_type=F32)
    act = (gate * jax.nn.sigmoid(gate) * up).astype(BF16)
    out_ref[...] += jnp.dot(act, wd_ref[...], preferred_element_type=F32)

    @pl.when(f == pl.num_programs(1) - 1)
    def _():
        out_ref[...] = h_ref[...] + _rms_scale(out_ref[...], gpost_ref[...])


def _ffn(h, g_pre, w_gate_up, w_down, g_post, layer, *, tm, tf):
    m = h.shape[0]
    nf = FFN_HIDDEN // tf
    return pl.pallas_call(
        _ffn_kernel,
        grid=(m // tm, nf),
        in_specs=[
            pl.BlockSpec((tm, D_MODEL), lambda i, f: (i, 0)),
            pl.BlockSpec((1, D_MODEL), lambda i, f: (0, 0)),
            pl.BlockSpec((None, D_MODEL, tf), lambda i, f: (layer, 0, f)),
            pl.BlockSpec((None, D_MODEL, tf), lambda i, f: (layer, 0, nf + f)),
            pl.BlockSpec((None, tf, D_MODEL), lambda i, f: (layer, f, 0)),
            pl.BlockSpec((1, D_MODEL), lambda i, f: (0, 0)),
        ],
        out_specs=pl.BlockSpec((tm, D_MODEL), lambda i, f: (i, 0)),
        out_shape=jax.ShapeDtypeStruct((m, D_MODEL), F32),
        scratch_shapes=[pltpu.VMEM((tm, D_MODEL), BF16)],
        compiler_params=pltpu.CompilerParams(
            dimension_semantics=("arbitrary", "arbitrary"),
            vmem_limit_bytes=V7X_VMEM_LIMIT),
        name="ffn",
    )(h, g_pre, w_gate_up, w_gate_up, w_down, g_post)


def _row_tile(seq_len, target):
    best = 16
    for t in range(16, target + 1, 16):
        if seq_len % t == 0:
            best = t
    return best


def kernel(x, meta_tokens, w_in, ml_conv_w, ml_igate_b, ml_fgate_b, pool_w, pool_scale, w_out,
           g_mix_pre, g_mix_post, g_ffn_pre, g_ffn_post, w_gate_up, w_down):
    batch, seq, d = x.shape
    depth = w_in.shape[0]
    seq_len = BLOCK + seq
    meta = jnp.broadcast_to(meta_tokens[None].astype(x.dtype), (batch, N_META, d))
    h = jnp.concatenate([jnp.zeros((batch, META_PAD, d), x.dtype), meta, x], axis=1)
    h = h.reshape(batch * seq_len, d)

    w_in_r = _w_in_prep(w_in)
    w_out_bf, pool_w_bf = w_out.astype(BF16), pool_w.astype(BF16)
    w_gate_up_bf, w_down_bf = w_gate_up.astype(BF16), w_down.astype(BF16)
    tm_mm = _row_tile(seq_len, 640)
    tm_out = _row_tile(seq_len, 640)
    tm_ffn = _row_tile(seq_len, 832)
    for l in range(depth):
        gate_bias = jnp.concatenate(
            [ml_igate_b[l], ml_fgate_b[l], jnp.zeros((GATE_W - N_GATES,), F32)])[None, :]
        u, gates = _in_proj(h, g_mix_pre[l][None, :], w_in_r, l, seq_len=seq_len, tm=tm_mm)
        ml, sb = _mixers(u, gates, ml_conv_w[l], gate_bias, batch=batch, seq_len=seq_len)
        h = _out_proj(ml, sb, u, h, w_out_bf, pool_w_bf, pool_scale[l], g_mix_post[l][None, :], l,
                      seq_len=seq_len, tm=tm_out)
        h = _ffn(h, g_ffn_pre[l][None, :], w_gate_up_bf, w_down_bf, g_ffn_post[l][None, :], l,
                 tm=tm_ffn, tf=512)
    return h.reshape(batch, seq_len, d)[:, BLOCK:, :]
```
